```python
import math
import jax, jax.numpy as jnp
from jax import lax
import numpy as np

D_MODEL = 2048
BATCH = 16
SEQ = 256
DEPTH = 4
DEC_BATCH = 8
DEC_SEQ = 2048
PAST_LEN = 512

GRID_W = 64
HEAD_DIM = 128
ATTN_Q_HEADS = 8
ATTN_KV_HEADS = 2
DIFF_HEADS = 4
DIFF_QK_DIM = 64
DIFF_V_DIM = 128
FOURIER_GROUPS = 4
FOURIER_GROUP_DIM = 128
MIX_WIDTH = ATTN_Q_HEADS * HEAD_DIM + DIFF_HEADS * DIFF_V_DIM + FOURIER_GROUPS * FOURIER_GROUP_DIM
D_FF = 4 * D_MODEL
ROPE_THETA = 10000.0
Q_BLOCK = 128
EPS = 1e-6
N_MOD = 6

W_QA = ATTN_Q_HEADS * HEAD_DIM
W_KA = ATTN_KV_HEADS * HEAD_DIM
W_VA = ATTN_KV_HEADS * HEAD_DIM
W_QB = DIFF_HEADS * 2 * DIFF_QK_DIM
W_KB = DIFF_HEADS * 2 * DIFF_QK_DIM
W_VB = DIFF_HEADS * DIFF_V_DIM
W_F = FOURIER_GROUPS * FOURIER_GROUP_DIM
IN_WIDTH = W_QA + W_KA + W_VA + W_QB + W_KB + W_VB + W_F

kernel_name = "hybrid_diffusion_prefix_trunk_step"


def rms_norm(x, g):
    xf = x.astype(jnp.float32)
    y = xf * lax.rsqrt(jnp.mean(jnp.square(xf), axis=-1, keepdims=True) + EPS)
    return (y * g.astype(jnp.float32)).astype(x.dtype)


def adaln(cond, w_ada_l, b_ada_l):
    m = jax.nn.silu(cond) @ w_ada_l + b_ada_l
    return jnp.split(m[:, None, :], N_MOD, axis=-1)


def axial_rope_tables(n_tokens, rot_dim):
    rows = n_tokens // GRID_W
    row = jnp.repeat(jnp.arange(rows), GRID_W).astype(jnp.float32)
    col = jnp.tile(jnp.arange(GRID_W), rows).astype(jnp.float32)
    nf = rot_dim // 4
    inv = ROPE_THETA ** (-jnp.arange(nf, dtype=jnp.float32) / nf)
    ang = jnp.concatenate([row[:, None] * inv, col[:, None] * inv], axis=-1)
    return jnp.cos(ang), jnp.sin(ang)


def apply_axial_rope(x, cos, sin):
    half = x.shape[-1] // 2
    nf = half // 2
    shape = (1, cos.shape[0]) + (1,) * (x.ndim - 3) + (half,)
    cos = cos.reshape(shape)
    sin = sin.reshape(shape)
    xf = x.astype(jnp.float32)

    def rot(xh, c, s):
        x1, x2 = xh[..., :nf], xh[..., nf:]
        return jnp.concatenate([x1 * c - x2 * s, x2 * c + x1 * s], axis=-1)

    out_r = rot(xf[..., :half], cos[..., :nf], sin[..., :nf])
    out_c = rot(xf[..., half:], cos[..., nf:], sin[..., nf:])
    return jnp.concatenate([out_r, out_c], axis=-1).astype(x.dtype)


def sweep_query_blocks(fn, q):
    b, t = q.shape[:2]
    nb = t // Q_BLOCK
    qb = jnp.moveaxis(q.reshape((b, nb, Q_BLOCK) + q.shape[2:]), 1, 0)
    ob = jnp.moveaxis(lax.map(fn, qb), 0, 1)
    return ob.reshape((b, t) + ob.shape[3:])


def gqa_attention(q, k, v):
    b, t, hq, d = q.shape
    hkv = k.shape[2]
    g = hq // hkv
    scale = d ** -0.5
    qg = q.reshape(b, t, hkv, g, d)

    def blk(qb):
        s = jnp.einsum('bqhgd,bkhd->bhgqk', qb, k).astype(jnp.float32) * scale
        p = jax.nn.softmax(s, axis=-1)
        return jnp.einsum('bhgqk,bkhd->bqhgd', p.astype(v.dtype), v)

    return sweep_query_blocks(blk, qg).reshape(b, t, hq * d)


def diff_attention(q, k, v, lam):
    scale = q.shape[-1] ** -0.5

    def blk(qb):
        s = jnp.einsum('bqhcd,bkhcd->bhcqk', qb, k).astype(jnp.float32) * scale
        p = jax.nn.softmax(s, axis=-1)
        pd = p[:, :, 0] - lam * p[:, :, 1]
        return jnp.einsum('bhqk,bkhd->bqhd', pd.astype(v.dtype), v)

    return sweep_query_blocks(blk, q)


def fourier_mix(f, w_fourier_l):
    b, t, _ = f.shape
    fg = f.reshape(b, t, FOURIER_GROUPS, FOURIER_GROUP_DIM).astype(jnp.float32)
    spec = jnp.fft.fft2(fg, axes=(1, 3), norm="ortho").real.astype(f.dtype)
    out = jnp.einsum('btgc,gcd->btgd', spec, w_fourier_l)
    return out.reshape(b, t, W_F)


def project(h, w_in_l, qn_a, kn_a, qn_b, kn_b):
    b, t, _ = h.shape
    z = h @ w_in_l
    i0 = W_QA
    i1 = i0 + W_KA
    i2 = i1 + W_VA
    i3 = i2 + W_QB
    i4 = i3 + W_KB
    i5 = i4 + W_VB
    qa = z[..., :i0].reshape(b, t, ATTN_Q_HEADS, HEAD_DIM)
    ka = z[..., i0:i1].reshape(b, t, ATTN_KV_HEADS, HEAD_DIM)
    va = z[..., i1:i2].reshape(b, t, ATTN_KV_HEADS, HEAD_DIM)
    qb = z[..., i2:i3].reshape(b, t, DIFF_HEADS, 2, DIFF_QK_DIM)
    kb = z[..., i3:i4].reshape(b, t, DIFF_HEADS, 2, DIFF_QK_DIM)
    vb = z[..., i4:i5].reshape(b, t, DIFF_HEADS, DIFF_V_DIM)
    f = z[..., i5:]
    return (rms_norm(qa, qn_a), rms_norm(ka, kn_a), va,
            rms_norm(qb, qn_b), rms_norm(kb, kn_b), vb, f)


def trunk_layer(x, cond, lambda_init, ctx, rope, lw):
    (w_ada_l, b_ada_l, norm_mix_g_l, norm_mlp_g_l, w_in_l, q_norm_a_l, k_norm_a_l,
     q_norm_b_l, k_norm_b_l, lambda_q1_l, lambda_k1_l, lambda_q2_l, lambda_k2_l,
     subln_g_l, w_fourier_l, w_out_l, w_mlp_in_l, w_mlp_out_l) = lw
    shift1, scale1, gate1, shift2, scale2, gate2 = adaln(cond, w_ada_l, b_ada_l)

    h = rms_norm(x, norm_mix_g_l) * (1.0 + scale1) + shift1
    qa, ka, va, qb, kb, vb, f = project(h, w_in_l, q_norm_a_l, k_norm_a_l, q_norm_b_l, k_norm_b_l)
    ctx_out = (ka, va, kb, vb)

    if ctx is None:
        keys_a, vals_a, keys_b, vals_b = ka, va, kb, vb
    else:
        (cos_a, sin_a), (cos_b, sin_b) = rope
        qa = apply_axial_rope(qa, cos_a, sin_a)
        qb = apply_axial_rope(qb, cos_b, sin_b)
        keys_a = jnp.concatenate([apply_axial_rope(ka, cos_a, sin_a), ctx[0].astype(ka.dtype)], axis=1)
        vals_a = jnp.concatenate([va, ctx[1].astype(va.dtype)], axis=1)
        keys_b = jnp.concatenate([apply_axial_rope(kb, cos_b, sin_b), ctx[2].astype(kb.dtype)], axis=1)
        vals_b = jnp.concatenate([vb, ctx[3].astype(vb.dtype)], axis=1)

    out_a = gqa_attention(qa, keys_a, vals_a)

    lam = (jnp.exp(jnp.sum(lambda_q1_l.astype(jnp.float32) * lambda_k1_l.astype(jnp.float32)))
           - jnp.exp(jnp.sum(lambda_q2_l.astype(jnp.float32) * lambda_k2_l.astype(jnp.float32)))
           + lambda_init)
    ob = diff_attention(qb, keys_b, vals_b, lam)
    ob = rms_norm(ob, subln_g_l) * (1.0 - lambda_init)
    out_b = ob.reshape(ob.shape[0], ob.shape[1], W_VB)

    out_c = fourier_mix(f, w_fourier_l)

    mix = jnp.concatenate([out_a, out_b, out_c], axis=-1) @ w_out_l
    x = x + gate1 * mix

    h2 = rms_norm(x, norm_mlp_g_l) * (1.0 + scale2) + shift2
    mlp = jnp.square(jax.nn.relu(h2 @ w_mlp_in_l)) @ w_mlp_out_l
    x = x + gate2 * mlp
    return x, ctx_out


def setup_inputs(seed: int = 0) -> dict:
    key = jax.random.key(seed)
    ks = jax.random.split(key, 32)
    f32 = jnp.float32
    nrm = lambda k, shape, s: jax.random.normal(k, shape, f32) * s
    gain = lambda k, shape: 1.0 + 0.02 * jax.random.normal(k, shape, f32)
    return {
        "x_prompt": nrm(ks[0], (BATCH, SEQ, D_MODEL), 1.0),
        "x_sample": nrm(ks[1], (DEC_BATCH, DEC_SEQ, D_MODEL), 1.0),
        "cache_attn_k": nrm(ks[2], (DEC_BATCH, DEPTH, PAST_LEN, ATTN_KV_HEADS, HEAD_DIM), 1.0),
        "cache_attn_v": nrm(ks[3], (DEC_BATCH, DEPTH, PAST_LEN, ATTN_KV_HEADS, HEAD_DIM), 1.0),
        "cache_diff_k": nrm(ks[4], (DEC_BATCH, DEPTH, PAST_LEN, DIFF_HEADS, 2, DIFF_QK_DIM), 1.0),
        "cache_diff_v": nrm(ks[5], (DEC_BATCH, DEPTH, PAST_LEN, DIFF_HEADS, DIFF_V_DIM), 1.0),
        "c": nrm(ks[6], (DEC_BATCH, D_MODEL), 1.0),
        "c_ctx": nrm(ks[7], (D_MODEL,), 1.0),
        "w_ada": nrm(ks[8], (DEPTH, D_MODEL, N_MOD * D_MODEL), D_MODEL ** -0.5),
        "b_ada": nrm(ks[9], (DEPTH, N_MOD * D_MODEL), 0.01),
        "norm_mix_g": gain(ks[10], (DEPTH, D_MODEL)),
        "norm_mlp_g": gain(ks[11], (DEPTH, D_MODEL)),
        "w_in": nrm(ks[12], (DEPTH, D_MODEL, IN_WIDTH), D_MODEL ** -0.5),
        "q_norm_a": gain(ks[13], (DEPTH, HEAD_DIM)),
        "k_norm_a": gain(ks[14], (DEPTH, HEAD_DIM)),
        "q_norm_b": gain(ks[15], (DEPTH, DIFF_QK_DIM)),
        "k_norm_b": gain(ks[16], (DEPTH, DIFF_QK_DIM)),
        "lambda_q1": nrm(ks[17], (DEPTH, DIFF_QK_DIM), 0.1),
        "lambda_k1": nrm(ks[18], (DEPTH, DIFF_QK_DIM), 0.1),
        "lambda_q2": nrm(ks[19], (DEPTH, DIFF_QK_DIM), 0.1),
        "lambda_k2": nrm(ks[20], (DEPTH, DIFF_QK_DIM), 0.1),
        "subln_g": gain(ks[21], (DEPTH, DIFF_V_DIM)),
        "w_fourier": nrm(ks[22], (DEPTH, FOURIER_GROUPS, FOURIER_GROUP_DIM, FOURIER_GROUP_DIM), FOURIER_GROUP_DIM ** -0.5),
        "w_out": nrm(ks[23], (DEPTH, MIX_WIDTH, D_MODEL), MIX_WIDTH ** -0.5),
        "w_mlp_in": nrm(ks[24], (DEPTH, D_MODEL, D_FF), D_MODEL ** -0.5),
        "w_mlp_out": nrm(ks[25], (DEPTH, D_FF, D_MODEL), D_FF ** -0.5),
    }


def reference(x_prompt, x_sample, cache_attn_k, cache_attn_v, cache_diff_k, cache_diff_v, c,
              c_ctx, w_ada, b_ada, norm_mix_g, norm_mlp_g, w_in, q_norm_a, k_norm_a, q_norm_b,
              k_norm_b, lambda_q1, lambda_k1, lambda_q2, lambda_k2, subln_g, w_fourier, w_out,
              w_mlp_in, w_mlp_out):
    n_lat = x_sample.shape[1]
    rope = (axial_rope_tables(n_lat, HEAD_DIM), axial_rope_tables(n_lat, DIFF_QK_DIM))
    cond_ctx = c_ctx[None, :]

    xp = x_prompt
    xs = x_sample
    ka_list, va_list, kb_list, vb_list = [], [], [], []
    for l in range(DEPTH):
        lw = (w_ada[l], b_ada[l], norm_mix_g[l], norm_mlp_g[l], w_in[l], q_norm_a[l], k_norm_a[l],
              q_norm_b[l], k_norm_b[l], lambda_q1[l], lambda_k1[l], lambda_q2[l], lambda_k2[l],
              subln_g[l], w_fourier[l], w_out[l], w_mlp_in[l], w_mlp_out[l])
        lambda_init = 0.8 - 0.6 * math.exp(-0.3 * l)
        xp, (ka, va, kb, vb) = trunk_layer(xp, cond_ctx, lambda_init, None, None, lw)
        ka_list.append(ka)
        va_list.append(va)
        kb_list.append(kb)
        vb_list.append(vb)
        ctx = (cache_attn_k[:, l], cache_attn_v[:, l], cache_diff_k[:, l], cache_diff_v[:, l])
        xs, _ = trunk_layer(xs, c, lambda_init, ctx, rope, lw)

    state_attn_k = jnp.stack(ka_list, axis=1)
    state_attn_v = jnp.stack(va_list, axis=1)
    state_diff_k = jnp.stack(kb_list, axis=1)
    state_diff_v = jnp.stack(vb_list, axis=1)
    return (xp, xs, state_attn_k, state_attn_v, state_diff_k, state_diff_v)
```

```python
import functools
import math

import jax
import jax.numpy as jnp
import numpy as np
from jax import lax
from jax.experimental import pallas as pl
from jax.experimental.pallas import tpu as pltpu

D_MODEL = 2048
BATCH = 16
SEQ = 256
DEPTH = 4
DEC_BATCH = 8
DEC_SEQ = 2048
PAST_LEN = 512
GRID_W = 64
HEAD_DIM = 128
ATTN_Q_HEADS = 8
ATTN_KV_HEADS = 2
DIFF_HEADS = 4
DIFF_QK_DIM = 64
DIFF_V_DIM = 128
FOURIER_GROUPS = 4
FOURIER_GROUP_DIM = 128
D_FF = 4 * D_MODEL
ROPE_THETA = 10000.0
EPS = 1e-6
N_MOD = 6

W_QA = ATTN_Q_HEADS * HEAD_DIM
W_KA = ATTN_KV_HEADS * HEAD_DIM
W_VA = ATTN_KV_HEADS * HEAD_DIM
W_QB = DIFF_HEADS * 2 * DIFF_QK_DIM
W_KB = DIFF_HEADS * 2 * DIFF_QK_DIM
W_VB = DIFF_HEADS * DIFF_V_DIM
W_F = FOURIER_GROUPS * FOURIER_GROUP_DIM
IN_WIDTH = W_QA + W_KA + W_VA + W_QB + W_KB + W_VB + W_F
MIX_WIDTH = W_QA + W_VB + W_F

LANES = 128
N_COND = 16
CTX_ROW = DEC_BATCH
VMEM_LIMIT_BYTES = 56 * 1024 * 1024

BF = jnp.bfloat16
F32 = jnp.float32


def _cparams(n_axes):
    return pltpu.CompilerParams(
        dimension_semantics=("arbitrary",) * n_axes, vmem_limit_bytes=VMEM_LIMIT_BYTES
    )


def _resident(block_shape, index_map):
    return pl.BlockSpec(block_shape, index_map, pipeline_mode=pl.Buffered(1))


def _adaln_kernel(c_ref, w_ref, b_ref, o_ref):
    c = c_ref[...]
    s = (c * jax.nn.sigmoid(c)).astype(BF)
    o_ref[...] = jnp.dot(s, w_ref[...].astype(BF), preferred_element_type=F32) + b_ref[...]


def _adaln_all(cond, w_ada, b_ada):
    tn = 1024
    n_out = N_MOD * D_MODEL
    return pl.pallas_call(
        _adaln_kernel,
        grid=(DEPTH, n_out // tn),
        in_specs=[
            pl.BlockSpec((N_COND, D_MODEL), lambda l, j: (0, 0)),
            pl.BlockSpec((None, D_MODEL, tn), lambda l, j: (l, 0, j)),
            pl.BlockSpec((None, 1, tn), lambda l, j: (l, 0, j)),
        ],
        out_specs=pl.BlockSpec((None, N_COND, tn), lambda l, j: (l, 0, j)),
        out_shape=jax.ShapeDtypeStruct((DEPTH, N_COND, n_out), F32),
        compiler_params=_cparams(2),
        name="adaln",
    )(cond, w_ada, b_ada.reshape(DEPTH, 1, n_out))


def _mod_spec(l, k, row_fn):
    return pl.BlockSpec((None, None, None, 1, D_MODEL), lambda i, *_: (l, row_fn(i), k, 0, 0))


def _modulated_norm(x, g, scale, shift):
    ms = jnp.mean(x * x, axis=-1, keepdims=True)
    h = (x * lax.rsqrt(ms + EPS)) * g
    return h * (1.0 + scale) + shift


def _group_rms(z, gain, group):
    zz = z * z
    if group == LANES:
        ms = jnp.sum(zz, axis=-1, keepdims=True) * (1.0 / LANES)
    else:
        lo = lax.broadcasted_iota(jnp.int32, (1, LANES), 1) < group
        a = jnp.sum(jnp.where(lo, zz, 0.0), axis=-1, keepdims=True)
        b = jnp.sum(jnp.where(lo, 0.0, zz), axis=-1, keepdims=True)
        ms = jnp.where(lo, a, b) * (1.0 / group)
    return (z * lax.rsqrt(ms + EPS)) * gain


def _rope(y, cos, sin, blk):
    lane = lax.broadcasted_iota(jnp.int32, (1, LANES), 1)
    first = (lane & (2 * blk - 1)) < blk
    partner = jnp.where(first, pltpu.roll(y, LANES - blk, 1), pltpu.roll(y, blk, 1))
    return y * cos + partner * sin


def _proj_kernel(latent, *refs):
    x_ref, sh_ref, sc_ref, g_ref, w_ref, qna_ref, kna_ref, qnb_ref, knb_ref = refs[:9]
    refs = refs[9:]
    if latent:
        cosa_ref, sina_ref, cosb_ref, sinb_ref = refs[:4]
        refs = refs[4:]
    qa_ref, ka_ref, va_ref, qb_ref, kb_ref, vb_ref, f_ref = refs[:7]
    if not latent:
        ska_ref, sva_ref, skb_ref, svb_ref = refs[7:]

    hb = _modulated_norm(x_ref[...], g_ref[...], sc_ref[...], sh_ref[...]).astype(BF)

    def cols(c0, width):
        return jnp.dot(hb, w_ref[:, c0:c0 + width], preferred_element_type=F32)

    def rope_a(y):
        return _rope(y, cosa_ref[...], sina_ref[...], HEAD_DIM // 4) if latent else y

    def rope_b(y):
        return _rope(y, cosb_ref[...], sinb_ref[...], DIFF_QK_DIM // 4) if latent else y

    def unit(z, u):
        return z[:, u * LANES:(u + 1) * LANES]

    def put(ref, u, val):
        ref[:, u * LANES:(u + 1) * LANES] = val.astype(ref.dtype)

    chunk = 4 * LANES
    c0 = 0
    for half in range(W_QA // chunk):
        z = cols(c0, chunk)
        for u in range(4):
            y = _group_rms(unit(z, u), qna_ref[...], HEAD_DIM)
            put(qa_ref, half * 4 + u, rope_a(y) * (HEAD_DIM ** -0.5))
        c0 += chunk

    z = cols(c0, W_KA + W_VA)
    for u in range(ATTN_KV_HEADS):
        y = _group_rms(unit(z, u), kna_ref[...], HEAD_DIM)
        if not latent:
            put(ska_ref, u, y)
        put(ka_ref, u, rope_a(y))
        v = unit(z, ATTN_KV_HEADS + u)
        if not latent:
            put(sva_ref, u, v)
        put(va_ref, u, v)
    c0 += W_KA + W_VA

    z = cols(c0, W_QB)
    for u in range(DIFF_HEADS):
        y = _group_rms(unit(z, u), qnb_ref[...], DIFF_QK_DIM)
        put(qb_ref, u, rope_b(y) * (DIFF_QK_DIM ** -0.5))
    c0 += W_QB

    z = cols(c0, W_KB)
    for u in range(DIFF_HEADS):
        y = _group_rms(unit(z, u), knb_ref[...], DIFF_QK_DIM)
        if not latent:
            put(skb_ref, u, y)
        put(kb_ref, u, rope_b(y))
    c0 += W_KB

    z = cols(c0, W_VB)
    if not latent:
        svb_ref[...] = z
    vb_ref[...] = z.astype(BF)
    c0 += W_VB

    f_ref[...] = cols(c0, W_F).astype(BF)


def _proj(x, mods, l, norm_g, w_in, qna, kna, qnb, knb, tables, *, latent, tm):
    n_tok = x.shape[0]
    per = DEC_SEQ // tm
    row_fn = (lambda i: i // per) if latent else (lambda i: CTX_ROW)
    vec = lambda n: pl.BlockSpec((None, 1, n), lambda i: (l, 0, 0))
    in_specs = [
        pl.BlockSpec((tm, D_MODEL), lambda i: (i, 0)),
        _mod_spec(l, 0, row_fn),
        _mod_spec(l, 1, row_fn),
        vec(D_MODEL),
        _resident((None, D_MODEL, IN_WIDTH), lambda i: (l, 0, 0)),
        vec(LANES), vec(LANES), vec(LANES), vec(LANES),
    ]
    args = [x, mods, mods, norm_g, w_in, qna, kna, qnb, knb]
    if latent:
        in_specs += [pl.BlockSpec((tm, LANES), lambda i: (i % per, 0))] * 4
        args += list(tables)
    widths = [W_QA, W_KA, W_VA, W_QB, W_KB, W_VB, W_F]
    out_specs = [pl.BlockSpec((tm, w), lambda i: (i, 0)) for w in widths]
    out_shape = [jax.ShapeDtypeStruct((n_tok, w), BF) for w in widths]
    if not latent:
        swidths = [W_KA, W_VA, W_KB, W_VB]
        out_specs += [pl.BlockSpec((tm, w), lambda i: (i, 0)) for w in swidths]
        out_shape += [jax.ShapeDtypeStruct((n_tok, w), F32) for w in swidths]
    return pl.pallas_call(
        functools.partial(_proj_kernel, latent),
        grid=(n_tok // tm,),
        in_specs=in_specs,
        out_specs=out_specs,
        out_shape=out_shape,
        compiler_params=_cparams(1),
        name="proj_lat" if latent else "proj_ctx",
    )(*args)


def _scores(q, k):
    return lax.dot_general(q, k, (((1,), (1,)), ((), ())), preferred_element_type=F32)


def _gqa_kernel(cached, *refs):
    if cached:
        q_ref, k_ref, v_ref, kc_ref, vc_ref, o_ref = refs
    else:
        q_ref, k_ref, v_ref, o_ref = refs
    q = q_ref[...]
    s = _scores(q, k_ref[...])
    m = jnp.max(s, axis=-1, keepdims=True)
    if cached:
        sc = _scores(q, kc_ref[...].astype(BF))
        m = jnp.maximum(m, jnp.max(sc, axis=-1, keepdims=True))
    p = jnp.exp(s - m)
    den = jnp.sum(p, axis=-1, keepdims=True)
    o = jnp.dot(p.astype(BF), v_ref[...], preferred_element_type=F32)
    if cached:
        pc = jnp.exp(sc - m)
        den = den + jnp.sum(pc, axis=-1, keepdims=True)
        o = o + jnp.dot(pc.astype(BF), vc_ref[...].astype(BF), preferred_element_type=F32)
    o_ref[...] = (o / den).astype(o_ref.dtype)


def _gqa(qa, ka, va, cache_k, cache_v, l, *, n_batch, seq, tq):
    cached = cache_k is not None
    nq = seq // tq
    grp = ATTN_Q_HEADS // ATTN_KV_HEADS
    in_specs = [
        pl.BlockSpec((tq, HEAD_DIM), lambda b, h, i: (b * nq + i, h)),
        pl.BlockSpec((seq, HEAD_DIM), lambda b, h, i: (b, h // grp)),
        pl.BlockSpec((seq, HEAD_DIM), lambda b, h, i: (b, h // grp)),
    ]
    args = [qa, ka, va]
    if cached:
        cspec = pl.BlockSpec((None, None, PAST_LEN, HEAD_DIM), lambda b, h, i: (b, l, 0, h // grp))
        in_specs += [cspec, cspec]
        args += [cache_k, cache_v]
    return pl.pallas_call(
        functools.partial(_gqa_kernel, cached),
        grid=(n_batch, ATTN_Q_HEADS, nq),
        in_specs=in_specs,
        out_specs=pl.BlockSpec((tq, HEAD_DIM), lambda b, h, i: (b * nq + i, h)),
        out_shape=jax.ShapeDtypeStruct((n_batch * seq, W_QA), BF),
        compiler_params=_cparams(3),
        name="gqa_lat" if cached else "gqa_ctx",
    )(*args)


def _diff_kernel(cached, lambda_init, *refs):
    lq1_ref, lk1_ref, lq2_ref, lk2_ref, sg_ref, q_ref, k_ref, v_ref = refs[:8]
    if cached:
        kc_ref, vc_ref, o_ref = refs[8:]
    else:
        (o_ref,) = refs[8:]
    lam = (jnp.exp(jnp.sum(lq1_ref[...] * lk1_ref[...], axis=-1, keepdims=True))
           - jnp.exp(jnp.sum(lq2_ref[...] * lk2_ref[...], axis=-1, keepdims=True))
           + lambda_init)

    q = q_ref[...]
    lo = lax.broadcasted_iota(jnp.int32, (1, LANES), 1) < DIFF_QK_DIM
    zero = jnp.zeros_like(q)
    qs = (jnp.where(lo, q, zero), jnp.where(lo, zero, q))
    k = k_ref[...]
    if cached:
        kc = kc_ref[...].astype(BF)
    o = None
    for c in range(2):
        s = _scores(qs[c], k)
        m = jnp.max(s, axis=-1, keepdims=True)
        if cached:
            sc = _scores(qs[c], kc)
            m = jnp.maximum(m, jnp.max(sc, axis=-1, keepdims=True))
        p = jnp.exp(s - m)
        den = jnp.sum(p, axis=-1, keepdims=True)
        oc = jnp.dot(p.astype(BF), v_ref[...], preferred_element_type=F32)
        if cached:
            pc = jnp.exp(sc - m)
            den = den + jnp.sum(pc, axis=-1, keepdims=True)
            oc = oc + jnp.dot(pc.astype(BF), vc_ref[...].astype(BF), preferred_element_type=F32)
        o = oc / den if c == 0 else o - lam * (oc / den)
    ms = jnp.mean(o * o, axis=-1, keepdims=True)
    y = (o * lax.rsqrt(ms + EPS)) * sg_ref[...]
    o_ref[...] = (y * (1.0 - lambda_init)).astype(o_ref.dtype)


def _diff(qb, kb, vb, cache_k, cache_v, lams, subln_g, l, lambda_init, *, n_batch, seq, tq):
    cached = cache_k is not None
    nq = seq // tq
    lvec = pl.BlockSpec((None, 1, DIFF_QK_DIM), lambda b, h, i: (l, 0, 0))
    in_specs = [lvec] * 4 + [
        pl.BlockSpec((None, 1, DIFF_V_DIM), lambda b, h, i: (l, 0, 0)),
        pl.BlockSpec((tq, LANES), lambda b, h, i: (b * nq + i, h)),
        pl.BlockSpec((seq, LANES), lambda b, h, i: (b, h)),
        pl.BlockSpec((seq, DIFF_V_DIM), lambda b, h, i: (b, h)),
    ]
    args = list(lams) + [subln_g, qb, kb, vb]
    if cached:
        cspec = pl.BlockSpec((None, None, PAST_LEN, LANES), lambda b, h, i: (b, l, 0, h))
        in_specs += [cspec, cspec]
        args += [cache_k, cache_v]
    return pl.pallas_call(
        functools.partial(_diff_kernel, cached, lambda_init),
        grid=(n_batch, DIFF_HEADS, nq),
        in_specs=in_specs,
        out_specs=pl.BlockSpec((tq, DIFF_V_DIM), lambda b, h, i: (b * nq + i, h)),
        out_shape=jax.ShapeDtypeStruct((n_batch * seq, W_VB), BF),
        compiler_params=_cparams(3),
        name="diff_lat" if cached else "diff_ctx",
    )(*args)


def _fourier_kernel(scale, f_ref, ct_ref, st_ref, cc_ref, sc_ref, w_ref, o_ref):
    f = f_ref[...]
    g1 = jnp.dot(f, cc_ref[...], preferred_element_type=F32).astype(BF)
    g2 = jnp.dot(f, sc_ref[...], preferred_element_type=F32).astype(BF)
    spec = (jnp.dot(ct_ref[...], g1, preferred_element_type=F32)
            - jnp.dot(st_ref[...], g2, preferred_element_type=F32)) * scale
    sb = spec.astype(BF)
    c = FOURIER_GROUP_DIM
    for g in range(FOURIER_GROUPS):
        o_ref[:, g * c:(g + 1) * c] = jnp.dot(
            sb[:, g * c:(g + 1) * c], w_ref[g], preferred_element_type=F32).astype(o_ref.dtype)


def _fourier(f, ct, st, cc, sc, w_fourier, l, *, n_batch, seq):
    scale = 1.0 / math.sqrt(seq * FOURIER_GROUP_DIM)
    c = FOURIER_GROUP_DIM
    return pl.pallas_call(
        functools.partial(_fourier_kernel, scale),
        grid=(n_batch,),
        in_specs=[
            pl.BlockSpec((seq, W_F), lambda b: (b, 0)),
            _resident((seq, seq), lambda b: (0, 0)),
            _resident((seq, seq), lambda b: (0, 0)),
            _resident((W_F, W_F), lambda b: (0, 0)),
            _resident((W_F, W_F), lambda b: (0, 0)),
            _resident((None, FOURIER_GROUPS, c, c), lambda b: (l, 0, 0, 0)),
        ],
        out_specs=pl.BlockSpec((seq, W_F), lambda b: (b, 0)),
        out_shape=jax.ShapeDtypeStruct((n_batch * seq, W_F), BF),
        compiler_params=_cparams(1),
        name="fourier_lat" if seq == DEC_SEQ else "fourier_ctx",
    )(f, ct, st, cc, sc, w_fourier)


def _dft_tables(n):
    k = np.arange(n, dtype=np.int64)
    ang = (2.0 * np.pi / n) * ((k[:, None] * k[None, :]) % n).astype(np.float64)
    return np.cos(ang), np.sin(ang)


def _oproj_kernel(x_ref, gt_ref, a_ref, b_ref, c_ref, w_ref, o_ref):
    acc = jnp.dot(a_ref[...], w_ref[0:W_QA, :], preferred_element_type=F32)
    acc += jnp.dot(b_ref[...], w_ref[W_QA:W_QA + W_VB, :], preferred_element_type=F32)
    acc += jnp.dot(c_ref[...], w_ref[W_QA + W_VB:MIX_WIDTH, :], preferred_element_type=F32)
    o_ref[...] = x_ref[...] + gt_ref[...] * acc


def _oproj(x, mods, l, out_a, out_b, out_c, w_out, *, latent, tm):
    n_tok = x.shape[0]
    per = DEC_SEQ // tm
    row_fn = (lambda i: i // per) if latent else (lambda i: CTX_ROW)
    return pl.pallas_call(
        _oproj_kernel,
        grid=(n_tok // tm,),
        in_specs=[
            pl.BlockSpec((tm, D_MODEL), lambda i: (i, 0)),
            _mod_spec(l, 2, row_fn),
            pl.BlockSpec((tm, W_QA), lambda i: (i, 0)),
            pl.BlockSpec((tm, W_VB), lambda i: (i, 0)),
            pl.BlockSpec((tm, W_F), lambda i: (i, 0)),
            _resident((None, MIX_WIDTH, D_MODEL), lambda i: (l, 0, 0)),
        ],
        out_specs=pl.BlockSpec((tm, D_MODEL), lambda i: (i, 0)),
        out_shape=jax.ShapeDtypeStruct((n_tok, D_MODEL), F32),
        compiler_params=_cparams(1),
        name="oproj_lat" if latent else "oproj_ctx",
    )(x, mods, out_a, out_b, out_c, w_out)


def _mlp_kernel(x_ref, sh_ref, sc_ref, gt_ref, g_ref, w1_ref, w2_ref, o_ref, h_scr, acc_scr):
    j = pl.program_id(1)

    @pl.when(j == 0)
    def _():
        h_scr[...] = _modulated_norm(x_ref[...], g_ref[...], sc_ref[...], sh_ref[...]).astype(BF)
        acc_scr[...] = jnp.zeros_like(acc_scr)

    hid = jnp.maximum(jnp.dot(h_scr[...], w1_ref[...], preferred_element_type=F32), 0.0)
    acc_scr[...] += jnp.dot((hid * hid).astype(BF), w2_ref[...], preferred_element_type=F32)

    @pl.when(j == pl.num_programs(1) - 1)
    def _():
        o_ref[...] = x_ref[...] + gt_ref[...] * acc_scr[...]


def _mlp(x, mods, l, norm_g, w1, w2, *, latent, tm, tf):
    n_tok = x.shape[0]
    per = DEC_SEQ // tm
    row_fn = (lambda i: i // per) if latent else (lambda i: CTX_ROW)
    return pl.pallas_call(
        _mlp_kernel,
        grid=(n_tok // tm, D_FF // tf),
        in_specs=[
            pl.BlockSpec((tm, D_MODEL), lambda i, j: (i, 0)),
            _mod_spec(l, 3, row_fn),
            _mod_spec(l, 4, row_fn),
            _mod_spec(l, 5, row_fn),
            pl.BlockSpec((None, 1, D_MODEL), lambda i, j: (l, 0, 0)),
            pl.BlockSpec((None, D_MODEL, tf), lambda i, j: (l, 0, j)),
            pl.BlockSpec((None, tf, D_MODEL), lambda i, j: (l, j, 0)),
        ],
        out_specs=pl.BlockSpec((tm, D_MODEL), lambda i, j: (i, 0)),
        out_shape=jax.ShapeDtypeStruct((n_tok, D_MODEL), F32),
        scratch_shapes=[pltpu.VMEM((tm, D_MODEL), BF), pltpu.VMEM((tm, D_MODEL), F32)],
        compiler_params=_cparams(2),
        name="mlp_lat" if latent else "mlp_ctx",
    )(x, mods, mods, mods, norm_g, w1, w2)


def _rope_tables(n_tokens, rot_dim):
    rows = n_tokens // GRID_W
    row = jnp.repeat(jnp.arange(rows), GRID_W).astype(F32)
    col = jnp.tile(jnp.arange(GRID_W), rows).astype(F32)
    nf = rot_dim // 4
    inv = ROPE_THETA ** (-jnp.arange(nf, dtype=F32) / nf)
    ar, ac = row[:, None] * inv, col[:, None] * inv
    cos = jnp.concatenate([jnp.cos(ar), jnp.cos(ar), jnp.cos(ac), jnp.cos(ac)], axis=-1)
    sin = jnp.concatenate([-jnp.sin(ar), jnp.sin(ar), -jnp.sin(ac), jnp.sin(ac)], axis=-1)
    reps = LANES // rot_dim
    return jnp.tile(cos, (1, reps)), jnp.tile(sin, (1, reps))


def _block_diag(m, n):
    return np.kron(np.eye(n), m)


def kernel(x_prompt, x_sample, cache_attn_k, cache_attn_v, cache_diff_k, cache_diff_v, c, c_ctx, w_ada, b_ada, norm_mix_g, norm_mlp_g, w_in, q_norm_a, k_norm_a, q_norm_b, k_norm_b, lambda_q1, lambda_k1, lambda_q2, lambda_k2, subln_g, w_fourier, w_out, w_mlp_in, w_mlp_out):
    n_ctx, n_lat = BATCH * SEQ, DEC_BATCH * DEC_SEQ
    xp = x_prompt.reshape(n_ctx, D_MODEL)
    xs = x_sample.reshape(n_lat, D_MODEL)

    cond = jnp.zeros((N_COND, D_MODEL), F32).at[:DEC_BATCH].set(c).at[CTX_ROW].set(c_ctx)
    mods = _adaln_all(cond, w_ada, b_ada).reshape(DEPTH, N_COND, N_MOD, 1, D_MODEL)

    w_in_b = w_in.astype(BF)
    w_out_b = w_out.astype(BF)
    w1_b = w_mlp_in.astype(BF)
    w2_b = w_mlp_out.astype(BF)
    wf_b = w_fourier.astype(BF)

    row3 = lambda a: a.reshape(DEPTH, 1, a.shape[-1])
    norm_mix = row3(norm_mix_g)
    norm_mlp = row3(norm_mlp_g)
    qna, kna = row3(q_norm_a), row3(k_norm_a)
    qnb = row3(jnp.tile(q_norm_b, (1, LANES // DIFF_QK_DIM)))
    knb = row3(jnp.tile(k_norm_b, (1, LANES // DIFF_QK_DIM)))
    lams = [row3(a) for a in (lambda_q1, lambda_k1, lambda_q2, lambda_k2)]
    subg = row3(subln_g)

    tables = _rope_tables(DEC_SEQ, HEAD_DIM) + _rope_tables(DEC_SEQ, DIFF_QK_DIM)

    cch, sch = _dft_tables(FOURIER_GROUP_DIM)
    cc = jnp.asarray(_block_diag(cch, FOURIER_GROUPS), BF)
    sc = jnp.asarray(_block_diag(sch, FOURIER_GROUPS), BF)
    dft = {n: tuple(jnp.asarray(t, BF) for t in _dft_tables(n)) for n in (SEQ, DEC_SEQ)}

    ck_a = cache_attn_k.reshape(DEC_BATCH, DEPTH, PAST_LEN, W_KA)
    cv_a = cache_attn_v.reshape(DEC_BATCH, DEPTH, PAST_LEN, W_VA)
    ck_b = cache_diff_k.reshape(DEC_BATCH, DEPTH, PAST_LEN, W_KB)
    cv_b = cache_diff_v.reshape(DEC_BATCH, DEPTH, PAST_LEN, W_VB)

    def layer(x, l, latent):
        lambda_init = 0.8 - 0.6 * math.exp(-0.3 * l)
        n_batch, seq = (DEC_BATCH, DEC_SEQ) if latent else (BATCH, SEQ)
        tm = 512 if latent else 256
        outs = _proj(x, mods, l, norm_mix, w_in_b, qna, kna, qnb, knb, tables, latent=latent, tm=tm)
        qa, ka, va, qb, kb, vb, f = outs[:7]
        caches = (ck_a, cv_a, ck_b, cv_b) if latent else (None,) * 4
        out_a = _gqa(qa, ka, va, caches[0], caches[1], l, n_batch=n_batch, seq=seq, tq=tm)
        out_b = _diff(qb, kb, vb, caches[2], caches[3], lams, subg, l, lambda_init,
                      n_batch=n_batch, seq=seq, tq=256)
        out_c = _fourier(f, dft[seq][0], dft[seq][1], cc, sc, wf_b, l, n_batch=n_batch, seq=seq)
        x = _oproj(x, mods, l, out_a, out_b, out_c, w_out_b, latent=latent, tm=tm)
        x = _mlp(x, mods, l, norm_mlp, w1_b, w2_b, latent=latent, tm=512, tf=1024)
        return x, outs[7:]

    states = []
    for l in range(DEPTH):
        xp, st = layer(xp, l, False)
        states.append(st)
        xs, _ = layer(xs, l, True)

    def stack(k, shape):
        return jnp.stack([s[k].reshape((BATCH, SEQ) + shape) for s in states], axis=1)

    return (
        xp.reshape(BATCH, SEQ, D_MODEL),
        xs.reshape(DEC_BATCH, DEC_SEQ, D_MODEL),
        stack(0, (ATTN_KV_HEADS, HEAD_DIM)),
        stack(1, (ATTN_KV_HEADS, HEAD_DIM)),
        stack(2, (DIFF_HEADS, 2, DIFF_QK_DIM)),
        stack(3, (DIFF_HEADS, DIFF_V_DIM)),
    )
```

```python
import functools
import math

import jax
import jax.numpy as jnp
import numpy as np
from jax import lax
from jax.experimental import pallas as pl
from jax.experimental.pallas import tpu as pltpu

D_MODEL = 2048
BATCH = 16
SEQ = 256
DEPTH = 4
DEC_BATCH = 8
DEC_SEQ = 2048
PAST_LEN = 512
GRID_W = 64
HEAD_DIM = 128
ATTN_Q_HEADS = 8
ATTN_KV_HEADS = 2
DIFF_HEADS = 4
DIFF_QK_DIM = 64
DIFF_V_DIM = 128
FOURIER_GROUPS = 4
FOURIER_GROUP_DIM = 128
D_FF = 4 * D_MODEL
ROPE_THETA = 10000.0
EPS = 1e-6
N_MOD = 6

W_QA = ATTN_Q_HEADS * HEAD_DIM
W_KA = ATTN_KV_HEADS * HEAD_DIM
W_VA = ATTN_KV_HEADS * HEAD_DIM
W_QB = DIFF_HEADS * 2 * DIFF_QK_DIM
W_KB = DIFF_HEADS * 2 * DIFF_QK_DIM
W_VB = DIFF_HEADS * DIFF_V_DIM
W_F = FOURIER_GROUPS * FOURIER_GROUP_DIM
IN_WIDTH = W_QA + W_KA + W_VA + W_QB + W_KB + W_VB + W_F
MIX_WIDTH = W_QA + W_VB + W_F

LANES = 128
NORM_ROWS = 16
ATTN_ROWS = 128
LOG2E = math.log2(math.e)
N_COND = 16
CTX_ROW = DEC_BATCH
VMEM_LIMIT_BYTES = 56 * 1024 * 1024

BF = jnp.bfloat16
F32 = jnp.float32


def _cparams(n_axes):
    return pltpu.CompilerParams(
        dimension_semantics=("arbitrary",) * n_axes, vmem_limit_bytes=VMEM_LIMIT_BYTES
    )


def _resident(block_shape, index_map):
    return pl.BlockSpec(block_shape, index_map, pipeline_mode=pl.Buffered(1))


def _adaln_kernel(c_ref, w_ref, b_ref, o_ref):
    c = c_ref[...]
    s = (c * jax.nn.sigmoid(c)).astype(BF)
    o_ref[...] = jnp.dot(s, w_ref[...].astype(BF), preferred_element_type=F32) + b_ref[...]


def _adaln_all(cond, w_ada, b_ada):
    tn = 1024
    n_out = N_MOD * D_MODEL
    return pl.pallas_call(
        _adaln_kernel,
        grid=(DEPTH, n_out // tn),
        in_specs=[
            pl.BlockSpec((N_COND, D_MODEL), lambda l, j: (0, 0)),
            pl.BlockSpec((None, D_MODEL, tn), lambda l, j: (l, 0, j)),
            pl.BlockSpec((None, 1, tn), lambda l, j: (l, 0, j)),
        ],
        out_specs=pl.BlockSpec((None, N_COND, tn), lambda l, j: (l, 0, j)),
        out_shape=jax.ShapeDtypeStruct((DEPTH, N_COND, n_out), F32),
        compiler_params=_cparams(2),
        name="adaln",
    )(cond, w_ada, b_ada.reshape(DEPTH, 1, n_out))


def _mod_spec(l, k, row_fn):
    return pl.BlockSpec((None, None, None, 1, D_MODEL), lambda i, *_: (l, row_fn(i), k, 0, 0))


def _modulated_norm_store(x_ref, g_ref, sc_ref, sh_ref, h_ref):
    g = g_ref[...]
    sc = 1.0 + sc_ref[...]
    sh = sh_ref[...]

    def body(r, carry):
        rows = pl.ds(pl.multiple_of(r * NORM_ROWS, NORM_ROWS), NORM_ROWS)
        x = x_ref[rows, :]
        ms = jnp.mean(x * x, axis=-1, keepdims=True)
        h = (x * lax.rsqrt(ms + EPS)) * g
        h_ref[rows, :] = (h * sc + sh).astype(h_ref.dtype)
        return carry

    lax.fori_loop(0, x_ref.shape[0] // NORM_ROWS, body, 0, unroll=4)


def _group_rms(z, gain, group):
    zz = z * z
    if group == LANES:
        ms = jnp.sum(zz, axis=-1, keepdims=True) * (1.0 / LANES)
    else:
        lo = lax.broadcasted_iota(jnp.int32, (1, LANES), 1) < group
        a = jnp.sum(jnp.where(lo, zz, 0.0), axis=-1, keepdims=True)
        b = jnp.sum(jnp.where(lo, 0.0, zz), axis=-1, keepdims=True)
        ms = jnp.where(lo, a, b) * (1.0 / group)
    return (z * lax.rsqrt(ms + EPS)) * gain


def _rope(y, cos, sin, blk):
    lane = lax.broadcasted_iota(jnp.int32, (1, LANES), 1)
    first = (lane & (2 * blk - 1)) < blk
    partner = jnp.where(first, pltpu.roll(y, LANES - blk, 1), pltpu.roll(y, blk, 1))
    return y * cos + partner * sin


def _proj_kernel(latent, *refs):
    x_ref, sh_ref, sc_ref, g_ref, w_ref, qna_ref, kna_ref, qnb_ref, knb_ref = refs[:9]
    refs = refs[9:]
    if latent:
        cosa_ref, sina_ref, cosb_ref, sinb_ref = refs[:4]
        refs = refs[4:]
    qa_ref, ka_ref, va_ref, qb_ref, kb_ref, vb_ref, f_ref = refs[:7]
    if not latent:
        ska_ref, sva_ref, skb_ref, svb_ref = refs[7:]

    x = x_ref[...]
    ms = jnp.mean(x * x, axis=-1, keepdims=True)
    h = (x * lax.rsqrt(ms + EPS)) * g_ref[...]
    hb = (h * (1.0 + sc_ref[...]) + sh_ref[...]).astype(BF)

    def cols(c0, width):
        return jnp.dot(hb, w_ref[:, c0:c0 + width], preferred_element_type=F32)

    def rope_a(y):
        return _rope(y, cosa_ref[...], sina_ref[...], HEAD_DIM // 4) if latent else y

    def rope_b(y):
        return _rope(y, cosb_ref[...], sinb_ref[...], DIFF_QK_DIM // 4) if latent else y

    def unit(z, u):
        return z[:, u * LANES:(u + 1) * LANES]

    def put(ref, u, val):
        ref[:, u * LANES:(u + 1) * LANES] = val.astype(ref.dtype)

    chunk = 4 * LANES
    c0 = 0
    for half in range(W_QA // chunk):
        z = cols(c0, chunk)
        for u in range(4):
            y = _group_rms(unit(z, u), qna_ref[...], HEAD_DIM)
            put(qa_ref, half * 4 + u, rope_a(y) * (HEAD_DIM ** -0.5 * LOG2E))
        c0 += chunk

    z = cols(c0, W_KA + W_VA)
    for u in range(ATTN_KV_HEADS):
        y = _group_rms(unit(z, u), kna_ref[...], HEAD_DIM)
        if not latent:
            put(ska_ref, u, y)
        put(ka_ref, u, rope_a(y))
        v = unit(z, ATTN_KV_HEADS + u)
        if not latent:
            put(sva_ref, u, v)
        put(va_ref, u, v)
    c0 += W_KA + W_VA

    z = cols(c0, W_QB)
    for u in range(DIFF_HEADS):
        y = _group_rms(unit(z, u), qnb_ref[...], DIFF_QK_DIM)
        put(qb_ref, u, rope_b(y) * (DIFF_QK_DIM ** -0.5 * LOG2E))
    c0 += W_QB

    z = cols(c0, W_KB)
    for u in range(DIFF_HEADS):
        y = _group_rms(unit(z, u), knb_ref[...], DIFF_QK_DIM)
        if not latent:
            put(skb_ref, u, y)
        put(kb_ref, u, rope_b(y))
    c0 += W_KB

    z = cols(c0, W_VB)
    if not latent:
        svb_ref[...] = z
    vb_ref[...] = z.astype(BF)
    c0 += W_VB

    f_ref[...] = cols(c0, W_F).astype(BF)


def _proj(x, mods, l, norm_g, w_in, qna, kna, qnb, knb, tables, *, latent, tm):
    n_tok = x.shape[0]
    per = DEC_SEQ // tm
    row_fn = (lambda i: i // per) if latent else (lambda i: CTX_ROW)
    vec = lambda n: pl.BlockSpec((None, 1, n), lambda i: (l, 0, 0))
    in_specs = [
        pl.BlockSpec((tm, D_MODEL), lambda i: (i, 0)),
        _mod_spec(l, 0, row_fn),
        _mod_spec(l, 1, row_fn),
        vec(D_MODEL),
        _resident((None, D_MODEL, IN_WIDTH), lambda i: (l, 0, 0)),
        vec(LANES), vec(LANES), vec(LANES), vec(LANES),
    ]
    args = [x, mods, mods, norm_g, w_in, qna, kna, qnb, knb]
    if latent:
        in_specs += [pl.BlockSpec((tm, LANES), lambda i: (i % per, 0))] * 4
        args += list(tables)
    widths = [W_QA, W_KA, W_VA, W_QB, W_KB, W_VB, W_F]
    out_specs = [pl.BlockSpec((tm, w), lambda i: (i, 0)) for w in widths]
    out_shape = [jax.ShapeDtypeStruct((n_tok, w), BF) for w in widths]
    if not latent:
        swidths = [W_KA, W_VA, W_KB, W_VB]
        out_specs += [pl.BlockSpec((tm, w), lambda i: (i, 0)) for w in swidths]
        out_shape += [jax.ShapeDtypeStruct((n_tok, w), F32) for w in swidths]
    return pl.pallas_call(
        functools.partial(_proj_kernel, latent),
        grid=(n_tok // tm,),
        in_specs=in_specs,
        out_specs=out_specs,
        out_shape=out_shape,
        compiler_params=_cparams(1),
        name="proj_lat" if latent else "proj_ctx",
    )(*args)


def _scores(q, k):
    return lax.dot_general(q, k, (((1,), (1,)), ((), ())), preferred_element_type=F32)


def _with_ones(v):
    return jnp.concatenate([v, jnp.ones_like(v)], axis=1)


def _softmax_pv(q, ks, vs):
    ss = [_scores(q, k) for k in ks]
    m = functools.reduce(jnp.maximum, [jnp.max(s, axis=-1, keepdims=True) for s in ss])
    o = functools.reduce(
        jnp.add,
        [jnp.dot(jnp.exp2(s - m).astype(BF), v, preferred_element_type=F32) for s, v in zip(ss, vs)])
    return o[:, :LANES] / o[:, LANES:LANES + 1]


def _kv_sources(unit, k_ref, v_ref, cache_refs):
    ks, vs = [k_ref[:, unit]], [_with_ones(v_ref[:, unit])]
    if cache_refs:
        kc_ref, vc_ref = cache_refs
        ks.append(kc_ref[:, unit].astype(BF))
        vs.append(_with_ones(vc_ref[:, unit].astype(BF)))
    return ks, vs


def _gqa_kernel(q_ref, k_ref, v_ref, *refs):
    *cache_refs, o_ref = refs
    n_kv = k_ref.shape[1] // HEAD_DIM
    grp = q_ref.shape[1] // HEAD_DIM // n_kv
    for kv in range(n_kv):
        ks, vs = _kv_sources(slice(kv * HEAD_DIM, (kv + 1) * HEAD_DIM), k_ref, v_ref, cache_refs)
        for h in range(kv * grp, (kv + 1) * grp):
            unit = slice(h * HEAD_DIM, (h + 1) * HEAD_DIM)
            for r in range(0, q_ref.shape[0], ATTN_ROWS):
                rows = slice(r, r + ATTN_ROWS)
                o_ref[rows, unit] = _softmax_pv(q_ref[rows, unit], ks, vs).astype(o_ref.dtype)


def _gqa(qa, ka, va, cache_k, cache_v, l, *, n_batch, seq):
    cached = cache_k is not None
    grp = ATTN_Q_HEADS // ATTN_KV_HEADS
    if cached:
        n_steps = ATTN_Q_HEADS
        qspec = pl.BlockSpec((seq, HEAD_DIM), lambda b, h: (b, h))
        kvspec = pl.BlockSpec((seq, HEAD_DIM), lambda b, h: (b, h // grp))
        cspec = pl.BlockSpec((None, None, PAST_LEN, HEAD_DIM), lambda b, h: (b, l, 0, h // grp))
        in_specs = [qspec, kvspec, kvspec, cspec, cspec]
        args = [qa, ka, va, cache_k, cache_v]
    else:
        n_steps = 1
        qspec = pl.BlockSpec((seq, W_QA), lambda b, h: (b, 0))
        kvspec = pl.BlockSpec((seq, W_KA), lambda b, h: (b, 0))
        in_specs = [qspec, kvspec, kvspec]
        args = [qa, ka, va]
    return pl.pallas_call(
        _gqa_kernel,
        grid=(n_batch, n_steps),
        in_specs=in_specs,
        out_specs=qspec,
        out_shape=jax.ShapeDtypeStruct((n_batch * seq, W_QA), BF),
        compiler_params=_cparams(2),
        name="gqa_lat" if cached else "gqa_ctx",
    )(*args)


def _diff_kernel(lambda_init, lq1_ref, lk1_ref, lq2_ref, lk2_ref, sg_ref, q_ref, k_ref, v_ref, *refs):
    *cache_refs, o_ref = refs
    lam = (jnp.exp(jnp.sum(lq1_ref[...] * lk1_ref[...], axis=-1, keepdims=True))
           - jnp.exp(jnp.sum(lq2_ref[...] * lk2_ref[...], axis=-1, keepdims=True))
           + lambda_init)
    lo = lax.broadcasted_iota(jnp.int32, (1, LANES), 1) < DIFF_QK_DIM
    for h in range(q_ref.shape[1] // LANES):
        unit = slice(h * LANES, (h + 1) * LANES)
        ks, vs = _kv_sources(unit, k_ref, v_ref, cache_refs)
        for r in range(0, q_ref.shape[0], ATTN_ROWS):
            rows = slice(r, r + ATTN_ROWS)
            q = q_ref[rows, unit]
            zero = jnp.zeros_like(q)
            o = _softmax_pv(jnp.where(lo, q, zero), ks, vs) - lam * _softmax_pv(jnp.where(lo, zero, q), ks, vs)
            ms = jnp.mean(o * o, axis=-1, keepdims=True)
            y = (o * lax.rsqrt(ms + EPS)) * sg_ref[...]
            o_ref[rows, unit] = (y * (1.0 - lambda_init)).astype(o_ref.dtype)


def _diff(qb, kb, vb, cache_k, cache_v, lams, subln_g, l, lambda_init, *, n_batch, seq):
    cached = cache_k is not None
    lvec = pl.BlockSpec((None, 1, DIFF_QK_DIM), lambda b, h: (l, 0, 0))
    in_specs = [lvec] * 4 + [pl.BlockSpec((None, 1, DIFF_V_DIM), lambda b, h: (l, 0, 0))]
    args = list(lams) + [subln_g, qb, kb, vb]
    if cached:
        n_steps = DIFF_HEADS
        spec = pl.BlockSpec((seq, LANES), lambda b, h: (b, h))
        cspec = pl.BlockSpec((None, None, PAST_LEN, LANES), lambda b, h: (b, l, 0, h))
        in_specs += [spec] * 3 + [cspec] * 2
        args += [cache_k, cache_v]
    else:
        n_steps = 1
        spec = pl.BlockSpec((seq, W_VB), lambda b, h: (b, 0))
        in_specs += [spec] * 3
    return pl.pallas_call(
        functools.partial(_diff_kernel, lambda_init),
        grid=(n_batch, n_steps),
        in_specs=in_specs,
        out_specs=spec,
        out_shape=jax.ShapeDtypeStruct((n_batch * seq, W_VB), BF),
        compiler_params=_cparams(2),
        name="diff_lat" if cached else "diff_ctx",
    )(*args)


def _fourier_kernel(scale, f_ref, ct_ref, st_ref, cc_ref, sc_ref, w_ref, o_ref):
    f = f_ref[...]
    g1 = jnp.dot(f, cc_ref[...], preferred_element_type=F32).astype(BF)
    g2 = jnp.dot(f, sc_ref[...], preferred_element_type=F32).astype(BF)
    spec = (jnp.dot(ct_ref[...], g1, preferred_element_type=F32)
            - jnp.dot(st_ref[...], g2, preferred_element_type=F32)) * scale
    sb = spec.astype(BF)
    c = FOURIER_GROUP_DIM
    for g in range(FOURIER_GROUPS):
        o_ref[:, g * c:(g + 1) * c] = jnp.dot(
            sb[:, g * c:(g + 1) * c], w_ref[g], preferred_element_type=F32).astype(o_ref.dtype)


def _fourier(f, ct, st, cc, sc, w_fourier, l, *, n_batch, seq):
    scale = 1.0 / math.sqrt(seq * FOURIER_GROUP_DIM)
    c = FOURIER_GROUP_DIM
    return pl.pallas_call(
        functools.partial(_fourier_kernel, scale),
        grid=(n_batch,),
        in_specs=[
            pl.BlockSpec((seq, W_F), lambda b: (b, 0)),
            _resident((seq, seq), lambda b: (0, 0)),
            _resident((seq, seq), lambda b: (0, 0)),
            _resident((W_F, W_F), lambda b: (0, 0)),
            _resident((W_F, W_F), lambda b: (0, 0)),
            _resident((None, FOURIER_GROUPS, c, c), lambda b: (l, 0, 0, 0)),
        ],
        out_specs=pl.BlockSpec((seq, W_F), lambda b: (b, 0)),
        out_shape=jax.ShapeDtypeStruct((n_batch * seq, W_F), BF),
        compiler_params=_cparams(1),
        name="fourier_lat" if seq == DEC_SEQ else "fourier_ctx",
    )(f, ct, st, cc, sc, w_fourier)


def _dft_tables(n):
    k = np.arange(n, dtype=np.int64)
    ang = (2.0 * np.pi / n) * ((k[:, None] * k[None, :]) % n).astype(np.float64)
    return np.cos(ang), np.sin(ang)


def _oproj_kernel(x_ref, gt_ref, a_ref, b_ref, c_ref, w_ref, o_ref):
    acc = jnp.dot(a_ref[...], w_ref[0:W_QA, :], preferred_element_type=F32)
    acc += jnp.dot(b_ref[...], w_ref[W_QA:W_QA + W_VB, :], preferred_element_type=F32)
    acc += jnp.dot(c_ref[...], w_ref[W_QA + W_VB:MIX_WIDTH, :], preferred_element_type=F32)
    o_ref[...] = x_ref[...] + gt_ref[...] * acc


def _oproj(x, mods, l, out_a, out_b, out_c, w_out, *, latent, tm):
    n_tok = x.shape[0]
    per = DEC_SEQ // tm
    row_fn = (lambda i: i // per) if latent else (lambda i: CTX_ROW)
    return pl.pallas_call(
        _oproj_kernel,
        grid=(n_tok // tm,),
        in_specs=[
            pl.BlockSpec((tm, D_MODEL), lambda i: (i, 0)),
            _mod_spec(l, 2, row_fn),
            pl.BlockSpec((tm, W_QA), lambda i: (i, 0)),
            pl.BlockSpec((tm, W_VB), lambda i: (i, 0)),
            pl.BlockSpec((tm, W_F), lambda i: (i, 0)),
            _resident((None, MIX_WIDTH, D_MODEL), lambda i: (l, 0, 0)),
        ],
        out_specs=pl.BlockSpec((tm, D_MODEL), lambda i: (i, 0)),
        out_shape=jax.ShapeDtypeStruct((n_tok, D_MODEL), F32),
        compiler_params=_cparams(1),
        name="oproj_lat" if latent else "oproj_ctx",
    )(x, mods, out_a, out_b, out_c, w_out)


def _mlp_kernel(x_ref, sh_ref, sc_ref, gt_ref, g_ref, w1_ref, w2_ref, o_ref, h_scr, acc_scr):
    j = pl.program_id(1)

    @pl.when(j == 0)
    def _():
        _modulated_norm_store(x_ref, g_ref, sc_ref, sh_ref, h_scr)
        acc_scr[...] = jnp.zeros_like(acc_scr)

    hid = jnp.maximum(jnp.dot(h_scr[...], w1_ref[...], preferred_element_type=F32), 0.0)
    acc_scr[...] += jnp.dot((hid * hid).astype(BF), w2_ref[...], preferred_element_type=F32)

    @pl.when(j == pl.num_programs(1) - 1)
    def _():
        o_ref[...] = x_ref[...] + gt_ref[...] * acc_scr[...]


def _mlp(x, mods, l, norm_g, w1, w2, *, latent, tm, tf):
    n_tok = x.shape[0]
    per = DEC_SEQ // tm
    row_fn = (lambda i: i // per) if latent else (lambda i: CTX_ROW)
    return pl.pallas_call(
        _mlp_kernel,
        grid=(n_tok // tm, D_FF // tf),
        in_specs=[
            pl.BlockSpec((tm, D_MODEL), lambda i, j: (i, 0)),
            _mod_spec(l, 3, row_fn),
            _mod_spec(l, 4, row_fn),
            _mod_spec(l, 5, row_fn),
            pl.BlockSpec((None, 1, D_MODEL), lambda i, j: (l, 0, 0)),
            pl.BlockSpec((None, D_MODEL, tf), lambda i, j: (l, 0, j)),
            pl.BlockSpec((None, tf, D_MODEL), lambda i, j: (l, j, 0)),
        ],
        out_specs=pl.BlockSpec((tm, D_MODEL), lambda i, j: (i, 0)),
        out_shape=jax.ShapeDtypeStruct((n_tok, D_MODEL), F32),
        scratch_shapes=[pltpu.VMEM((tm, D_MODEL), BF), pltpu.VMEM((tm, D_MODEL), F32)],
        compiler_params=_cparams(2),
        name="mlp_lat" if latent else "mlp_ctx",
    )(x, mods, mods, mods, norm_g, w1, w2)


def _rope_tables(n_tokens, rot_dim):
    rows = n_tokens // GRID_W
    row = jnp.repeat(jnp.arange(rows), GRID_W).astype(F32)
    col = jnp.tile(jnp.arange(GRID_W), rows).astype(F32)
    nf = rot_dim // 4
    inv = ROPE_THETA ** (-jnp.arange(nf, dtype=F32) / nf)
    ar, ac = row[:, None] * inv, col[:, None] * inv
    cos = jnp.concatenate([jnp.cos(ar), jnp.cos(ar), jnp.cos(ac), jnp.cos(ac)], axis=-1)
    sin = jnp.concatenate([-jnp.sin(ar), jnp.sin(ar), -jnp.sin(ac), jnp.sin(ac)], axis=-1)
    reps = LANES // rot_dim
    return jnp.tile(cos, (1, reps)), jnp.tile(sin, (1, reps))


def _block_diag(m, n):
    return np.kron(np.eye(n), m)


def kernel(x_prompt, x_sample, cache_attn_k, cache_attn_v, cache_diff_k, cache_diff_v, c, c_ctx, w_ada, b_ada, norm_mix_g, norm_mlp_g, w_in, q_norm_a, k_norm_a, q_norm_b, k_norm_b, lambda_q1, lambda_k1, lambda_q2, lambda_k2, subln_g, w_fourier, w_out, w_mlp_in, w_mlp_out):
    n_ctx, n_lat = BATCH * SEQ, DEC_BATCH * DEC_SEQ
    xp = x_prompt.reshape(n_ctx, D_MODEL)
    xs = x_sample.reshape(n_lat, D_MODEL)

    cond = jnp.zeros((N_COND, D_MODEL), F32).at[:DEC_BATCH].set(c).at[CTX_ROW].set(c_ctx)
    mods = _adaln_all(cond, w_ada, b_ada).reshape(DEPTH, N_COND, N_MOD, 1, D_MODEL)

    w_in_b = w_in.astype(BF)
    w_out_b = w_out.astype(BF)
    w1_b = w_mlp_in.astype(BF)
    w2_b = w_mlp_out.astype(BF)
    wf_b = w_fourier.astype(BF)

    row3 = lambda a: a.reshape(DEPTH, 1, a.shape[-1])
    norm_mix = row3(norm_mix_g)
    norm_mlp = row3(norm_mlp_g)
    qna, kna = row3(q_norm_a), row3(k_norm_a)
    qnb = row3(jnp.tile(q_norm_b, (1, LANES // DIFF_QK_DIM)))
    knb = row3(jnp.tile(k_norm_b, (1, LANES // DIFF_QK_DIM)))
    lams = [row3(a) for a in (lambda_q1, lambda_k1, lambda_q2, lambda_k2)]
    subg = row3(subln_g)

    tables = _rope_tables(DEC_SEQ, HEAD_DIM) + _rope_tables(DEC_SEQ, DIFF_QK_DIM)

    cch, sch = _dft_tables(FOURIER_GROUP_DIM)
    cc = jnp.asarray(_block_diag(cch, FOURIER_GROUPS), BF)
    sc = jnp.asarray(_block_diag(sch, FOURIER_GROUPS), BF)
    dft = {n: tuple(jnp.asarray(t, BF) for t in _dft_tables(n)) for n in (SEQ, DEC_SEQ)}

    ck_a = cache_attn_k.reshape(DEC_BATCH, DEPTH, PAST_LEN, W_KA)
    cv_a = cache_attn_v.reshape(DEC_BATCH, DEPTH, PAST_LEN, W_VA)
    ck_b = cache_diff_k.reshape(DEC_BATCH, DEPTH, PAST_LEN, W_KB)
    cv_b = cache_diff_v.reshape(DEC_BATCH, DEPTH, PAST_LEN, W_VB)

    def layer(x, l, latent):
        lambda_init = 0.8 - 0.6 * math.exp(-0.3 * l)
        n_batch, seq = (DEC_BATCH, DEC_SEQ) if latent else (BATCH, SEQ)
        tm = 512 if latent else 256
        outs = _proj(x, mods, l, norm_mix, w_in_b, qna, kna, qnb, knb, tables, latent=latent, tm=tm)
        qa, ka, va, qb, kb, vb, f = outs[:7]
        caches = (ck_a, cv_a, ck_b, cv_b) if latent else (None,) * 4
        out_a = _gqa(qa, ka, va, caches[0], caches[1], l, n_batch=n_batch, seq=seq)
        out_b = _diff(qb, kb, vb, caches[2], caches[3], lams, subg, l, lambda_init, n_batch=n_batch, seq=seq)
        out_c = _fourier(f, dft[seq][0], dft[seq][1], cc, sc, wf_b, l, n_batch=n_batch, seq=seq)
        x = _oproj(x, mods, l, out_a, out_b, out_c, w_out_b, latent=latent, tm=tm)
        x = _mlp(x, mods, l, norm_mlp, w1_b, w2_b, latent=latent, tm=512, tf=1024)
        return x, outs[7:]

    states = []
    for l in range(DEPTH):
        xp, st = layer(xp, l, False)
        states.append(st)
        xs, _ = layer(xs, l, True)

    def stack(k, shape):
        return jnp.stack([s[k].reshape((BATCH, SEQ) + shape) for s in states], axis=1)

    return (
        xp.reshape(BATCH, SEQ, D_MODEL),
        xs.reshape(DEC_BATCH, DEC_SEQ, D_MODEL),
        stack(0, (ATTN_KV_HEADS, HEAD_DIM)),
        stack(1, (ATTN_KV_HEADS, HEAD_DIM)),
        stack(2, (DIFF_HEADS, 2, DIFF_QK_DIM)),
        stack(3, (DIFF_HEADS, DIFF_V_DIM)),
    )
```

```python
import functools
import math

import jax
import jax.numpy as jnp
import numpy as np
from jax import lax
from jax.experimental import pallas as pl
from jax.experimental.pallas import tpu as pltpu

D_MODEL = 2048
BATCH = 16
SEQ = 256
DEPTH = 4
DEC_BATCH = 8
DEC_SEQ = 2048
PAST_LEN = 512
GRID_W = 64
HEAD_DIM = 128
ATTN_Q_HEADS = 8
ATTN_KV_HEADS = 2
DIFF_HEADS = 4
DIFF_QK_DIM = 64
DIFF_V_DIM = 128
FOURIER_GROUPS = 4
FOURIER_GROUP_DIM = 128
D_FF = 4 * D_MODEL
ROPE_THETA = 10000.0
EPS = 1e-6
N_MOD = 6

W_QA = ATTN_Q_HEADS * HEAD_DIM
W_KA = ATTN_KV_HEADS * HEAD_DIM
W_VA = ATTN_KV_HEADS * HEAD_DIM
W_QB = DIFF_HEADS * 2 * DIFF_QK_DIM
W_KB = DIFF_HEADS * 2 * DIFF_QK_DIM
W_VB = DIFF_HEADS * DIFF_V_DIM
W_F = FOURIER_GROUPS * FOURIER_GROUP_DIM
IN_WIDTH = W_QA + W_KA + W_VA + W_QB + W_KB + W_VB + W_F
MIX_WIDTH = W_QA + W_VB + W_F

LANES = 128
NORM_ROWS = 16
ATTN_ROWS = 128
CTX_ATTN_SEQS = 4
LOG2E = math.log2(math.e)
N_COND = 16
CTX_ROW = DEC_BATCH
VMEM_LIMIT_BYTES = 56 * 1024 * 1024

BF = jnp.bfloat16
F32 = jnp.float32


def _cparams(n_axes):
    return pltpu.CompilerParams(
        dimension_semantics=("arbitrary",) * n_axes, vmem_limit_bytes=VMEM_LIMIT_BYTES
    )


def _resident(block_shape, index_map):
    return pl.BlockSpec(block_shape, index_map, pipeline_mode=pl.Buffered(1))


def _adaln_kernel(c_ref, w_ref, b_ref, o_ref):
    c = c_ref[...]
    s = (c * jax.nn.sigmoid(c)).astype(BF)
    o_ref[...] = jnp.dot(s, w_ref[...].astype(BF), preferred_element_type=F32) + b_ref[...]


def _adaln_all(cond, w_ada, b_ada):
    tn = 1024
    n_out = N_MOD * D_MODEL
    return pl.pallas_call(
        _adaln_kernel,
        grid=(DEPTH, n_out // tn),
        in_specs=[
            pl.BlockSpec((N_COND, D_MODEL), lambda l, j: (0, 0)),
            pl.BlockSpec((None, D_MODEL, tn), lambda l, j: (l, 0, j)),
            pl.BlockSpec((None, 1, tn), lambda l, j: (l, 0, j)),
        ],
        out_specs=pl.BlockSpec((None, N_COND, tn), lambda l, j: (l, 0, j)),
        out_shape=jax.ShapeDtypeStruct((DEPTH, N_COND, n_out), F32),
        compiler_params=_cparams(2),
        name="adaln",
    )(cond, w_ada, b_ada.reshape(DEPTH, 1, n_out))


def _mod_spec(l, k, row_fn):
    return pl.BlockSpec((None, None, None, 1, D_MODEL), lambda i, *_: (l, row_fn(i), k, 0, 0))


def _modulated_norm(x, g, scale, shift):
    ms = jnp.mean(x * x, axis=-1, keepdims=True)
    h = (x * lax.rsqrt(ms + EPS)) * g
    return (h * (1.0 + scale) + shift).astype(BF)


def _modulated_norm_store(x_ref, g_ref, sc_ref, sh_ref, h_ref):
    def body(r, carry):
        rows = pl.ds(pl.multiple_of(r * NORM_ROWS, NORM_ROWS), NORM_ROWS)
        h_ref[rows, :] = _modulated_norm(x_ref[rows, :], g_ref[...], sc_ref[...], sh_ref[...])
        return carry

    lax.fori_loop(0, x_ref.shape[0] // NORM_ROWS, body, 0, unroll=4)


def _by_parity(step, body, buf_a, buf_b):
    @pl.when(step % 2 == 0)
    def _():
        body(buf_a, buf_b)

    @pl.when(step % 2 == 1)
    def _():
        body(buf_b, buf_a)


def _group_rms(z, gain, group):
    zz = z * z
    if group == LANES:
        ms = jnp.sum(zz, axis=-1, keepdims=True) * (1.0 / LANES)
    else:
        lo = lax.broadcasted_iota(jnp.int32, (1, LANES), 1) < group
        a = jnp.sum(jnp.where(lo, zz, 0.0), axis=-1, keepdims=True)
        b = jnp.sum(jnp.where(lo, 0.0, zz), axis=-1, keepdims=True)
        ms = jnp.where(lo, a, b) * (1.0 / group)
    return (z * lax.rsqrt(ms + EPS)) * gain


def _rope(y, cos, sin, blk):
    lane = lax.broadcasted_iota(jnp.int32, (1, LANES), 1)
    first = (lane & (2 * blk - 1)) < blk
    partner = jnp.where(first, pltpu.roll(y, LANES - blk, 1), pltpu.roll(y, blk, 1))
    return y * cos + partner * sin


def _proj_kernel(latent, *refs):
    x_ref, sh_ref, sc_ref, g_ref, w_ref, qna_ref, kna_ref, qnb_ref, knb_ref = refs[:9]
    refs = refs[9:]
    if latent:
        cosa_ref, sina_ref, cosb_ref, sinb_ref = refs[:4]
        refs = refs[4:]
    qa_ref, ka_ref, va_ref, qb_ref, kb_ref, vb_ref, f_ref = refs[:7]
    if not latent:
        ska_ref, sva_ref, skb_ref, svb_ref = refs[7:]

    hb = _modulated_norm(x_ref[...], g_ref[...], sc_ref[...], sh_ref[...])

    def cols(c0, width):
        return jnp.dot(hb, w_ref[:, c0:c0 + width], preferred_element_type=F32)

    def rope_a(y):
        return _rope(y, cosa_ref[...], sina_ref[...], HEAD_DIM // 4) if latent else y

    def rope_b(y):
        return _rope(y, cosb_ref[...], sinb_ref[...], DIFF_QK_DIM // 4) if latent else y

    def unit(z, u):
        return z[:, u * LANES:(u + 1) * LANES]

    def put(ref, u, val):
        ref[:, u * LANES:(u + 1) * LANES] = val.astype(ref.dtype)

    c_qa, c_ka = 0, W_QA
    c_qb = c_ka + W_KA + W_VA
    c_kb = c_qb + W_QB
    c_vb = c_kb + W_KB
    c_f = c_vb + W_VB

    z = cols(c_qb, W_QB)
    for u in range(DIFF_HEADS):
        y = _group_rms(unit(z, u), qnb_ref[...], DIFF_QK_DIM)
        put(qb_ref, u, rope_b(y) * (DIFF_QK_DIM ** -0.5 * LOG2E))

    z = cols(c_kb, W_KB)
    for u in range(DIFF_HEADS):
        y = _group_rms(unit(z, u), knb_ref[...], DIFF_QK_DIM)
        if not latent:
            put(skb_ref, u, y)
        put(kb_ref, u, rope_b(y))

    chunk = 4 * LANES
    for half in range(W_QA // chunk):
        z = cols(c_qa + half * chunk, chunk)
        for u in range(4):
            y = _group_rms(unit(z, u), qna_ref[...], HEAD_DIM)
            put(qa_ref, half * 4 + u, rope_a(y) * (HEAD_DIM ** -0.5 * LOG2E))

    z = cols(c_ka, W_KA + W_VA)
    for u in range(ATTN_KV_HEADS):
        y = _group_rms(unit(z, u), kna_ref[...], HEAD_DIM)
        if not latent:
            put(ska_ref, u, y)
        put(ka_ref, u, rope_a(y))
        v = unit(z, ATTN_KV_HEADS + u)
        if not latent:
            put(sva_ref, u, v)
        put(va_ref, u, v)

    z = cols(c_vb, W_VB)
    if not latent:
        svb_ref[...] = z
    vb_ref[...] = z.astype(BF)

    f_ref[...] = cols(c_f, W_F).astype(BF)


def _proj(x, mods, l, norm_g, w_in, qna, kna, qnb, knb, tables, *, latent, tm):
    n_tok = x.shape[0]
    n_blk = n_tok // tm
    per = DEC_SEQ // tm
    row_fn = (lambda i: i // per) if latent else (lambda i: CTX_ROW)
    vec = lambda n: pl.BlockSpec((None, 1, n), lambda i: (l, 0, 0))
    in_specs = [
        pl.BlockSpec((tm, D_MODEL), lambda i: (i, 0)),
        _mod_spec(l, 0, row_fn),
        _mod_spec(l, 1, row_fn),
        vec(D_MODEL),
        _resident((None, D_MODEL, IN_WIDTH), lambda i: (l, 0, 0)),
        vec(LANES), vec(LANES), vec(LANES), vec(LANES),
    ]
    args = [x, mods, mods, norm_g, w_in, qna, kna, qnb, knb]
    if latent:
        in_specs += [pl.BlockSpec((tm, LANES), lambda i: (i % per, 0))] * 4
        args += list(tables)
    widths = [W_QA, W_KA, W_VA, W_QB, W_KB, W_VB, W_F]
    out_specs = [pl.BlockSpec((tm, w), lambda i: (i, 0)) for w in widths]
    out_shape = [jax.ShapeDtypeStruct((n_tok, w), BF) for w in widths]
    if not latent:
        swidths = [W_KA, W_VA, W_KB, W_VB]
        out_specs += [pl.BlockSpec((tm, w), lambda i: (i, 0)) for w in swidths]
        out_shape += [jax.ShapeDtypeStruct((n_tok, w), F32) for w in swidths]
    return pl.pallas_call(
        functools.partial(_proj_kernel, latent),
        grid=(n_blk,),
        in_specs=in_specs,
        out_specs=out_specs,
        out_shape=out_shape,
        compiler_params=_cparams(1),
        name="proj_lat" if latent else "proj_ctx",
    )(*args)


def _scores(q, k):
    return lax.dot_general(q, k, (((1,), (1,)), ((), ())), preferred_element_type=F32)


def _with_ones(v):
    return jnp.concatenate([v, jnp.ones_like(v)], axis=1)


def _softmax_pv(q, ks, vs):
    ss = [_scores(q, k) for k in ks]
    m = functools.reduce(jnp.maximum, [jnp.max(s, axis=-1, keepdims=True) for s in ss])
    o = functools.reduce(
        jnp.add,
        [jnp.dot(jnp.exp2(s - m).astype(BF), v, preferred_element_type=F32) for s, v in zip(ss, vs)])
    return o[:, :LANES] / o[:, LANES:LANES + 1]


def _kv_sources(keys, unit, k_ref, v_ref, cache_refs):
    ks, vs = [k_ref[keys, unit]], [_with_ones(v_ref[keys, unit])]
    if cache_refs:
        kc_ref, vc_ref = cache_refs
        ks.append(kc_ref[:, unit].astype(BF))
        vs.append(_with_ones(vc_ref[:, unit].astype(BF)))
    return ks, vs


def _gqa_kernel(seq, q_ref, k_ref, v_ref, *refs):
    *cache_refs, o_ref = refs
    n_kv = k_ref.shape[1] // HEAD_DIM
    grp = q_ref.shape[1] // HEAD_DIM // n_kv
    for s0 in range(0, q_ref.shape[0], seq):
        for kv in range(n_kv):
            ks, vs = _kv_sources(slice(s0, s0 + seq), slice(kv * HEAD_DIM, (kv + 1) * HEAD_DIM),
                                 k_ref, v_ref, cache_refs)
            for h in range(kv * grp, (kv + 1) * grp):
                unit = slice(h * HEAD_DIM, (h + 1) * HEAD_DIM)
                for r in range(s0, s0 + seq, ATTN_ROWS):
                    rows = slice(r, r + ATTN_ROWS)
                    o_ref[rows, unit] = _softmax_pv(q_ref[rows, unit], ks, vs).astype(o_ref.dtype)


def _gqa(qa, ka, va, cache_k, cache_v, l, *, n_batch, seq):
    cached = cache_k is not None
    grp = ATTN_Q_HEADS // ATTN_KV_HEADS
    if cached:
        grid = (n_batch, ATTN_Q_HEADS)
        qspec = pl.BlockSpec((seq, HEAD_DIM), lambda b, h: (b, h))
        kvspec = pl.BlockSpec((seq, HEAD_DIM), lambda b, h: (b, h // grp))
        cspec = pl.BlockSpec((None, None, PAST_LEN, HEAD_DIM), lambda b, h: (b, l, 0, h // grp))
        in_specs = [qspec, kvspec, kvspec, cspec, cspec]
        args = [qa, ka, va, cache_k, cache_v]
    else:
        grid = (n_batch // CTX_ATTN_SEQS, 1)
        qspec = pl.BlockSpec((CTX_ATTN_SEQS * seq, W_QA), lambda b, h: (b, 0))
        kvspec = pl.BlockSpec((CTX_ATTN_SEQS * seq, W_KA), lambda b, h: (b, 0))
        in_specs = [qspec, kvspec, kvspec]
        args = [qa, ka, va]
    return pl.pallas_call(
        functools.partial(_gqa_kernel, seq),
        grid=grid,
        in_specs=in_specs,
        out_specs=qspec,
        out_shape=jax.ShapeDtypeStruct((n_batch * seq, W_QA), BF),
        compiler_params=_cparams(2),
        name="gqa_lat" if cached else "gqa_ctx",
    )(*args)


def _diff_kernel(seq, lambda_init, lq1_ref, lk1_ref, lq2_ref, lk2_ref, sg_ref, q_ref, k_ref, v_ref, *refs):
    *cache_refs, o_ref = refs
    lam = (jnp.exp(jnp.sum(lq1_ref[...] * lk1_ref[...], axis=-1, keepdims=True))
           - jnp.exp(jnp.sum(lq2_ref[...] * lk2_ref[...], axis=-1, keepdims=True))
           + lambda_init)
    lo = lax.broadcasted_iota(jnp.int32, (1, LANES), 1) < DIFF_QK_DIM
    for s0 in range(0, q_ref.shape[0], seq):
        for h in range(q_ref.shape[1] // LANES):
            unit = slice(h * LANES, (h + 1) * LANES)
            ks, vs = _kv_sources(slice(s0, s0 + seq), unit, k_ref, v_ref, cache_refs)
            for r in range(s0, s0 + seq, ATTN_ROWS):
                rows = slice(r, r + ATTN_ROWS)
                q = q_ref[rows, unit]
                zero = jnp.zeros_like(q)
                o = _softmax_pv(jnp.where(lo, q, zero), ks, vs) - lam * _softmax_pv(jnp.where(lo, zero, q), ks, vs)
                ms = jnp.mean(o * o, axis=-1, keepdims=True)
                y = (o * lax.rsqrt(ms + EPS)) * sg_ref[...]
                o_ref[rows, unit] = (y * (1.0 - lambda_init)).astype(o_ref.dtype)


def _diff(qb, kb, vb, cache_k, cache_v, lams, subln_g, l, lambda_init, *, n_batch, seq):
    cached = cache_k is not None
    lvec = pl.BlockSpec((None, 1, DIFF_QK_DIM), lambda b, h: (l, 0, 0))
    in_specs = [lvec] * 4 + [pl.BlockSpec((None, 1, DIFF_V_DIM), lambda b, h: (l, 0, 0))]
    args = list(lams) + [subln_g, qb, kb, vb]
    if cached:
        grid = (n_batch, DIFF_HEADS)
        spec = pl.BlockSpec((seq, LANES), lambda b, h: (b, h))
        cspec = pl.BlockSpec((None, None, PAST_LEN, LANES), lambda b, h: (b, l, 0, h))
        in_specs += [spec] * 3 + [cspec] * 2
        args += [cache_k, cache_v]
    else:
        grid = (n_batch // CTX_ATTN_SEQS, 1)
        spec = pl.BlockSpec((CTX_ATTN_SEQS * seq, W_VB), lambda b, h: (b, 0))
        in_specs += [spec] * 3
    return pl.pallas_call(
        functools.partial(_diff_kernel, seq, lambda_init),
        grid=grid,
        in_specs=in_specs,
        out_specs=spec,
        out_shape=jax.ShapeDtypeStruct((n_batch * seq, W_VB), BF),
        compiler_params=_cparams(2),
        name="diff_lat" if cached else "diff_ctx",
    )(*args)


def _fourier_kernel(scale, f_ref, ct_ref, st_ref, cc_ref, sc_ref, w_ref, o_ref):
    f = f_ref[...]
    g1 = jnp.dot(f, cc_ref[...], preferred_element_type=F32).astype(BF)
    g2 = jnp.dot(f, sc_ref[...], preferred_element_type=F32).astype(BF)
    spec = (jnp.dot(ct_ref[...], g1, preferred_element_type=F32)
            - jnp.dot(st_ref[...], g2, preferred_element_type=F32)) * scale
    sb = spec.astype(BF)
    c = FOURIER_GROUP_DIM
    for g in range(FOURIER_GROUPS):
        o_ref[:, g * c:(g + 1) * c] = jnp.dot(
            sb[:, g * c:(g + 1) * c], w_ref[g], preferred_element_type=F32).astype(o_ref.dtype)


def _fourier(f, ct, st, cc, sc, w_fourier, l, *, n_batch, seq):
    scale = 1.0 / math.sqrt(seq * FOURIER_GROUP_DIM)
    c = FOURIER_GROUP_DIM
    return pl.pallas_call(
        functools.partial(_fourier_kernel, scale),
        grid=(n_batch,),
        in_specs=[
            pl.BlockSpec((seq, W_F), lambda b: (b, 0)),
            _resident((seq, seq), lambda b: (0, 0)),
            _resident((seq, seq), lambda b: (0, 0)),
            _resident((W_F, W_F), lambda b: (0, 0)),
            _resident((W_F, W_F), lambda b: (0, 0)),
            _resident((None, FOURIER_GROUPS, c, c), lambda b: (l, 0, 0, 0)),
        ],
        out_specs=pl.BlockSpec((seq, W_F), lambda b: (b, 0)),
        out_shape=jax.ShapeDtypeStruct((n_batch * seq, W_F), BF),
        compiler_params=_cparams(1),
        name="fourier_lat" if seq == DEC_SEQ else "fourier_ctx",
    )(f, ct, st, cc, sc, w_fourier)


def _dft_tables(n):
    k = np.arange(n, dtype=np.int64)
    ang = (2.0 * np.pi / n) * ((k[:, None] * k[None, :]) % n).astype(np.float64)
    return np.cos(ang), np.sin(ang)


def _oproj_kernel(x_ref, gt_ref, a_ref, b_ref, c_ref, w_ref, o_ref):
    acc = jnp.dot(a_ref[...], w_ref[0:W_QA, :], preferred_element_type=F32)
    acc += jnp.dot(b_ref[...], w_ref[W_QA:W_QA + W_VB, :], preferred_element_type=F32)
    acc += jnp.dot(c_ref[...], w_ref[W_QA + W_VB:MIX_WIDTH, :], preferred_element_type=F32)
    o_ref[...] = x_ref[...] + gt_ref[...] * acc


def _oproj(x, mods, l, out_a, out_b, out_c, w_out, *, latent, tm):
    n_tok = x.shape[0]
    per = DEC_SEQ // tm
    row_fn = (lambda i: i // per) if latent else (lambda i: CTX_ROW)
    return pl.pallas_call(
        _oproj_kernel,
        grid=(n_tok // tm,),
        in_specs=[
            pl.BlockSpec((tm, D_MODEL), lambda i: (i, 0)),
            _mod_spec(l, 2, row_fn),
            pl.BlockSpec((tm, W_QA), lambda i: (i, 0)),
            pl.BlockSpec((tm, W_VB), lambda i: (i, 0)),
            pl.BlockSpec((tm, W_F), lambda i: (i, 0)),
            _resident((None, MIX_WIDTH, D_MODEL), lambda i: (l, 0, 0)),
        ],
        out_specs=pl.BlockSpec((tm, D_MODEL), lambda i: (i, 0)),
        out_shape=jax.ShapeDtypeStruct((n_tok, D_MODEL), F32),
        compiler_params=_cparams(1),
        name="oproj_lat" if latent else "oproj_ctx",
    )(x, mods, out_a, out_b, out_c, w_out)


def _mlp_kernel(x_ref, sh_ref, sc_ref, gt_ref, xn_ref, shn_ref, scn_ref, g_ref, w1_ref, w2_ref, o_ref,
                ha_scr, hb_scr, acc_scr):
    i = pl.program_id(0)
    j = pl.program_id(1)
    n_j = pl.num_programs(1)
    rows_per_step = x_ref.shape[0] // (D_FF // w1_ref.shape[1])

    @pl.when((i == 0) & (j == 0))
    def _():
        _modulated_norm_store(x_ref, g_ref, sc_ref, sh_ref, ha_scr)

    @pl.when(j == 0)
    def _():
        acc_scr[...] = jnp.zeros_like(acc_scr)

    def body(h_cur, h_next):
        rows = pl.ds(pl.multiple_of(j * rows_per_step, rows_per_step), rows_per_step)
        h_next[rows, :] = _modulated_norm(xn_ref[rows, :], g_ref[...], scn_ref[...], shn_ref[...])
        hid = jnp.maximum(jnp.dot(h_cur[...], w1_ref[...], preferred_element_type=F32), 0.0)
        acc_scr[...] += jnp.dot((hid * hid).astype(BF), w2_ref[...], preferred_element_type=F32)

    _by_parity(i, body, ha_scr, hb_scr)

    @pl.when(j == n_j - 1)
    def _():
        o_ref[...] = x_ref[...] + gt_ref[...] * acc_scr[...]


def _mlp(x, mods, l, norm_g, w1, w2, *, latent, tm, tf):
    n_tok = x.shape[0]
    n_blk = n_tok // tm
    per = DEC_SEQ // tm
    row_fn = (lambda i: i // per) if latent else (lambda i: CTX_ROW)
    nxt = lambda i: jnp.minimum(i + 1, n_blk - 1)
    return pl.pallas_call(
        _mlp_kernel,
        grid=(n_blk, D_FF // tf),
        in_specs=[
            pl.BlockSpec((tm, D_MODEL), lambda i, j: (i, 0)),
            _mod_spec(l, 3, row_fn),
            _mod_spec(l, 4, row_fn),
            _mod_spec(l, 5, row_fn),
            pl.BlockSpec((tm, D_MODEL), lambda i, j: (nxt(i), 0)),
            _mod_spec(l, 3, lambda i: row_fn(nxt(i))),
            _mod_spec(l, 4, lambda i: row_fn(nxt(i))),
            pl.BlockSpec((None, 1, D_MODEL), lambda i, j: (l, 0, 0)),
            pl.BlockSpec((None, D_MODEL, tf), lambda i, j: (l, 0, j)),
            pl.BlockSpec((None, tf, D_MODEL), lambda i, j: (l, j, 0)),
        ],
        out_specs=pl.BlockSpec((tm, D_MODEL), lambda i, j: (i, 0)),
        out_shape=jax.ShapeDtypeStruct((n_tok, D_MODEL), F32),
        scratch_shapes=[pltpu.VMEM((tm, D_MODEL), BF)] * 2 + [pltpu.VMEM((tm, D_MODEL), F32)],
        compiler_params=_cparams(2),
        name="mlp_lat" if latent else "mlp_ctx",
    )(x, mods, mods, mods, x, mods, mods, norm_g, w1, w2)


def _rope_tables(n_tokens, rot_dim):
    rows = n_tokens // GRID_W
    row = jnp.repeat(jnp.arange(rows), GRID_W).astype(F32)
    col = jnp.tile(jnp.arange(GRID_W), rows).astype(F32)
    nf = rot_dim // 4
    inv = ROPE_THETA ** (-jnp.arange(nf, dtype=F32) / nf)
    ar, ac = row[:, None] * inv, col[:, None] * inv
    cos = jnp.concatenate([jnp.cos(ar), jnp.cos(ar), jnp.cos(ac), jnp.cos(ac)], axis=-1)
    sin = jnp.concatenate([-jnp.sin(ar), jnp.sin(ar), -jnp.sin(ac), jnp.sin(ac)], axis=-1)
    reps = LANES // rot_dim
    return jnp.tile(cos, (1, reps)), jnp.tile(sin, (1, reps))


def _block_diag(m, n):
    return np.kron(np.eye(n), m)


def kernel(x_prompt, x_sample, cache_attn_k, cache_attn_v, cache_diff_k, cache_diff_v, c, c_ctx, w_ada, b_ada, norm_mix_g, norm_mlp_g, w_in, q_norm_a, k_norm_a, q_norm_b, k_norm_b, lambda_q1, lambda_k1, lambda_q2, lambda_k2, subln_g, w_fourier, w_out, w_mlp_in, w_mlp_out):
    n_ctx, n_lat = BATCH * SEQ, DEC_BATCH * DEC_SEQ
    xp = x_prompt.reshape(n_ctx, D_MODEL)
    xs = x_sample.reshape(n_lat, D_MODEL)

    cond = jnp.zeros((N_COND, D_MODEL), F32).at[:DEC_BATCH].set(c).at[CTX_ROW].set(c_ctx)
    mods = _adaln_all(cond, w_ada, b_ada).reshape(DEPTH, N_COND, N_MOD, 1, D_MODEL)

    w_in_b = w_in.astype(BF)
    w_out_b = w_out.astype(BF)
    w1_b = w_mlp_in.astype(BF)
    w2_b = w_mlp_out.astype(BF)
    wf_b = w_fourier.astype(BF)

    row3 = lambda a: a.reshape(DEPTH, 1, a.shape[-1])
    norm_mix = row3(norm_mix_g)
    norm_mlp = row3(norm_mlp_g)
    qna, kna = row3(q_norm_a), row3(k_norm_a)
    qnb = row3(jnp.tile(q_norm_b, (1, LANES // DIFF_QK_DIM)))
    knb = row3(jnp.tile(k_norm_b, (1, LANES // DIFF_QK_DIM)))
    lams = [row3(a) for a in (lambda_q1, lambda_k1, lambda_q2, lambda_k2)]
    subg = row3(subln_g)

    tables = _rope_tables(DEC_SEQ, HEAD_DIM) + _rope_tables(DEC_SEQ, DIFF_QK_DIM)

    cch, sch = _dft_tables(FOURIER_GROUP_DIM)
    cc = jnp.asarray(_block_diag(cch, FOURIER_GROUPS), BF)
    sc = jnp.asarray(_block_diag(sch, FOURIER_GROUPS), BF)
    dft = {n: tuple(jnp.asarray(t, BF) for t in _dft_tables(n)) for n in (SEQ, DEC_SEQ)}

    ck_a = cache_attn_k.reshape(DEC_BATCH, DEPTH, PAST_LEN, W_KA)
    cv_a = cache_attn_v.reshape(DEC_BATCH, DEPTH, PAST_LEN, W_VA)
    ck_b = cache_diff_k.reshape(DEC_BATCH, DEPTH, PAST_LEN, W_KB)
    cv_b = cache_diff_v.reshape(DEC_BATCH, DEPTH, PAST_LEN, W_VB)

    def layer(x, l, latent):
        lambda_init = 0.8 - 0.6 * math.exp(-0.3 * l)
        n_batch, seq = (DEC_BATCH, DEC_SEQ) if latent else (BATCH, SEQ)
        tm = 512 if latent else 256
        outs = _proj(x, mods, l, norm_mix, w_in_b, qna, kna, qnb, knb, tables, latent=latent, tm=tm)
        qa, ka, va, qb, kb, vb, f = outs[:7]
        caches = (ck_a, cv_a, ck_b, cv_b) if latent else (None,) * 4
        out_a = _gqa(qa, ka, va, caches[0], caches[1], l, n_batch=n_batch, seq=seq)
        out_b = _diff(qb, kb, vb, caches[2], caches[3], lams, subg, l, lambda_init, n_batch=n_batch, seq=seq)
        out_c = _fourier(f, dft[seq][0], dft[seq][1], cc, sc, wf_b, l, n_batch=n_batch, seq=seq)
        x = _oproj(x, mods, l, out_a, out_b, out_c, w_out_b, latent=latent, tm=tm)
        x = _mlp(x, mods, l, norm_mlp, w1_b, w2_b, latent=latent, tm=512, tf=1024)
        return x, outs[7:]

    states = []
    for l in range(DEPTH):
        xp, st = layer(xp, l, False)
        states.append(st)
        xs, _ = layer(xs, l, True)

    def stack(k, shape):
        return jnp.stack([s[k].reshape((BATCH, SEQ) + shape) for s in states], axis=1)

    return (
        xp.reshape(BATCH, SEQ, D_MODEL),
        xs.reshape(DEC_BATCH, DEC_SEQ, D_MODEL),
        stack(0, (ATTN_KV_HEADS, HEAD_DIM)),
        stack(1, (ATTN_KV_HEADS, HEAD_DIM)),
        stack(2, (DIFF_HEADS, 2, DIFF_QK_DIM)),
        stack(3, (DIFF_HEADS, DIFF_V_DIM)),
    )
```

```python
import functools
import math

import jax
import jax.numpy as jnp
import numpy as np
from jax import lax
from jax.experimental import pallas as pl
from jax.experimental.pallas import tpu as pltpu

D_MODEL = 2048
BATCH = 16
SEQ = 256
DEPTH = 4
DEC_BATCH = 8
DEC_SEQ = 2048
PAST_LEN = 512
GRID_W = 64
HEAD_DIM = 128
ATTN_Q_HEADS = 8
ATTN_KV_HEADS = 2
DIFF_HEADS = 4
DIFF_QK_DIM = 64
DIFF_V_DIM = 128
FOURIER_GROUPS = 4
FOURIER_GROUP_DIM = 128
D_FF = 4 * D_MODEL
ROPE_THETA = 10000.0
EPS = 1e-6
N_MOD = 6

W_QA = ATTN_Q_HEADS * HEAD_DIM
W_KA = ATTN_KV_HEADS * HEAD_DIM
W_VA = ATTN_KV_HEADS * HEAD_DIM
W_QB = DIFF_HEADS * 2 * DIFF_QK_DIM
W_KB = DIFF_HEADS * 2 * DIFF_QK_DIM
W_VB = DIFF_HEADS * DIFF_V_DIM
W_F = FOURIER_GROUPS * FOURIER_GROUP_DIM
IN_WIDTH = W_QA + W_KA + W_VA + W_QB + W_KB + W_VB + W_F
MIX_WIDTH = W_QA + W_VB + W_F

LANES = 128
NORM_ROWS = 16
ATTN_ROWS = 128
CTX_ATTN_SEQS = 4
LOG2E = math.log2(math.e)
N_COND = 16
CTX_ROW = DEC_BATCH
VMEM_LIMIT_BYTES = 56 * 1024 * 1024

BF = jnp.bfloat16
F32 = jnp.float32


MLP_FF_BLOCK = 512


def _block_rows(latent):
    if latent:
        return {"proj": 512, "oproj": 512, "mlp": 1024}
    return {"proj": SEQ, "oproj": 512, "mlp": 1024}


def _cparams(n_axes):
    return pltpu.CompilerParams(
        dimension_semantics=("arbitrary",) * n_axes, vmem_limit_bytes=VMEM_LIMIT_BYTES
    )


def _resident(block_shape, index_map):
    return pl.BlockSpec(block_shape, index_map, pipeline_mode=pl.Buffered(1))


def _adaln_kernel(c_ref, w_ref, b_ref, o_ref):
    c = c_ref[...]
    s = (c * jax.nn.sigmoid(c)).astype(BF)
    o_ref[...] = jnp.dot(s, w_ref[...].astype(BF), preferred_element_type=F32) + b_ref[...]


def _adaln_all(cond, w_ada, b_ada):
    tn = 1024
    n_out = N_MOD * D_MODEL
    return pl.pallas_call(
        _adaln_kernel,
        grid=(DEPTH, n_out // tn),
        in_specs=[
            pl.BlockSpec((N_COND, D_MODEL), lambda l, j: (0, 0)),
            pl.BlockSpec((None, D_MODEL, tn), lambda l, j: (l, 0, j)),
            pl.BlockSpec((None, 1, tn), lambda l, j: (l, 0, j)),
        ],
        out_specs=pl.BlockSpec((None, N_COND, tn), lambda l, j: (l, 0, j)),
        out_shape=jax.ShapeDtypeStruct((DEPTH, N_COND, n_out), F32),
        compiler_params=_cparams(2),
        name="adaln",
    )(cond, w_ada, b_ada.reshape(DEPTH, 1, n_out))


def _mod_spec(l, k, row_fn):
    return pl.BlockSpec((None, None, None, 1, D_MODEL), lambda i, *_: (l, row_fn(i), k, 0, 0))


def _modulated_norm(x, g, scale, shift):
    ms = jnp.mean(x * x, axis=-1, keepdims=True)
    h = (x * lax.rsqrt(ms + EPS)) * g
    return (h * (1.0 + scale) + shift).astype(BF)


def _modulated_norm_store(x_ref, g_ref, sc_ref, sh_ref, h_ref):
    def body(r, carry):
        rows = pl.ds(pl.multiple_of(r * NORM_ROWS, NORM_ROWS), NORM_ROWS)
        h_ref[rows, :] = _modulated_norm(x_ref[rows, :], g_ref[...], sc_ref[...], sh_ref[...])
        return carry

    lax.fori_loop(0, x_ref.shape[0] // NORM_ROWS, body, 0, unroll=4)


def _group_rms(z, gain, group):
    zz = z * z
    if group == LANES:
        ms = jnp.sum(zz, axis=-1, keepdims=True) * (1.0 / LANES)
    else:
        lo = lax.broadcasted_iota(jnp.int32, (1, LANES), 1) < group
        a = jnp.sum(jnp.where(lo, zz, 0.0), axis=-1, keepdims=True)
        b = jnp.sum(jnp.where(lo, 0.0, zz), axis=-1, keepdims=True)
        ms = jnp.where(lo, a, b) * (1.0 / group)
    return (z * lax.rsqrt(ms + EPS)) * gain


def _rope(y, cos, sin, blk):
    lane = lax.broadcasted_iota(jnp.int32, (1, LANES), 1)
    first = (lane & (2 * blk - 1)) < blk
    partner = jnp.where(first, pltpu.roll(y, LANES - blk, 1), pltpu.roll(y, blk, 1))
    return y * cos + partner * sin


def _proj_kernel(latent, *refs):
    x_ref, sh_ref, sc_ref, g_ref, w_ref, qna_ref, kna_ref, qnb_ref, knb_ref = refs[:9]
    refs = refs[9:]
    if latent:
        cosa_ref, sina_ref, cosb_ref, sinb_ref = refs[:4]
    refs = refs[4:]
    qa_ref, ka_ref, va_ref, qb_ref, kb_ref, vb_ref, f_ref = refs[:7]
    if not latent:
        ska_ref, sva_ref, skb_ref, svb_ref = refs[7:]

    hb = _modulated_norm(x_ref[...], g_ref[...], sc_ref[...], sh_ref[...])

    def put_state(ref, u, val):
        ref[pl.ds(u, SEQ, stride=ref.shape[0] // SEQ), :] = val

    def cols(c0, width):
        return jnp.dot(hb, w_ref[:, c0:c0 + width], preferred_element_type=F32)

    def rope_a(y):
        return _rope(y, cosa_ref[...], sina_ref[...], HEAD_DIM // 4) if latent else y

    def rope_b(y):
        return _rope(y, cosb_ref[...], sinb_ref[...], DIFF_QK_DIM // 4) if latent else y

    def unit(z, u):
        return z[:, u * LANES:(u + 1) * LANES]

    def put(ref, u, val):
        ref[:, u * LANES:(u + 1) * LANES] = val.astype(ref.dtype)

    c_qa, c_ka = 0, W_QA
    c_qb = c_ka + W_KA + W_VA
    c_kb = c_qb + W_QB
    c_vb = c_kb + W_KB
    c_f = c_vb + W_VB

    z = cols(c_qb, W_QB)
    for u in range(DIFF_HEADS):
        y = _group_rms(unit(z, u), qnb_ref[...], DIFF_QK_DIM)
        put(qb_ref, u, rope_b(y) * (DIFF_QK_DIM ** -0.5 * LOG2E))

    z = cols(c_kb, W_KB)
    for u in range(DIFF_HEADS):
        y = _group_rms(unit(z, u), knb_ref[...], DIFF_QK_DIM)
        if not latent:
            put_state(skb_ref, u, y)
        put(kb_ref, u, rope_b(y))

    chunk = 4 * LANES
    for half in range(W_QA // chunk):
        z = cols(c_qa + half * chunk, chunk)
        for u in range(4):
            y = _group_rms(unit(z, u), qna_ref[...], HEAD_DIM)
            put(qa_ref, half * 4 + u, rope_a(y) * (HEAD_DIM ** -0.5 * LOG2E))

    z = cols(c_ka, W_KA + W_VA)
    for u in range(ATTN_KV_HEADS):
        y = _group_rms(unit(z, u), kna_ref[...], HEAD_DIM)
        if not latent:
            put_state(ska_ref, u, y)
        put(ka_ref, u, rope_a(y))
        v = unit(z, ATTN_KV_HEADS + u)
        if not latent:
            put_state(sva_ref, u, v)
        put(va_ref, u, v)

    z = cols(c_vb, W_VB)
    if not latent:
        for u in range(DIFF_HEADS):
            put_state(svb_ref, u, unit(z, u))
    vb_ref[...] = z.astype(BF)

    f_ref[...] = cols(c_f, W_F).astype(BF)


def _proj(x, mods, l, norm_g, w_in, qna, kna, qnb, knb, tables, *, latent, tm):
    n_tok = x.shape[0]
    n_blk = n_tok // tm
    per = DEC_SEQ // tm
    row_fn = (lambda i: i // per) if latent else (lambda i: CTX_ROW)
    vec = lambda n: pl.BlockSpec((None, 1, n), lambda i: (l, 0, 0))
    in_specs = [
        pl.BlockSpec((tm, D_MODEL), lambda i: (i, 0)),
        _mod_spec(l, 0, row_fn),
        _mod_spec(l, 1, row_fn),
        vec(D_MODEL),
        _resident((None, D_MODEL, IN_WIDTH), lambda i: (l, 0, 0)),
        vec(LANES), vec(LANES), vec(LANES), vec(LANES),
    ]
    args = [x, mods, mods, norm_g, w_in, qna, kna, qnb, knb]
    widths = [W_QA, W_KA, W_VA, W_QB, W_KB, W_VB, W_F]
    out_specs = [pl.BlockSpec((tm, w), lambda i: (i, 0)) for w in widths]
    out_shape = [jax.ShapeDtypeStruct((n_tok, w), BF) for w in widths]
    aliases = {}
    if latent:
        in_specs += [pl.BlockSpec((tm, LANES), lambda i: (i % per, 0))] * 4
        args += list(tables)
    else:
        assert tm == SEQ
        aliases = {len(args) + k: len(widths) + k for k in range(len(tables))}
        in_specs += [pl.BlockSpec(memory_space=pl.ANY)] * len(tables)
        args += list(tables)
        out_specs += [pl.BlockSpec((None, None) + s.shape[2:], lambda i: (i, l, 0, 0)) for s in tables]
        out_shape += [jax.ShapeDtypeStruct(s.shape, s.dtype) for s in tables]
    return pl.pallas_call(
        functools.partial(_proj_kernel, latent),
        grid=(n_blk,),
        in_specs=in_specs,
        out_specs=out_specs,
        out_shape=out_shape,
        input_output_aliases=aliases,
        compiler_params=_cparams(1),
        name="proj_lat" if latent else "proj_ctx",
    )(*args)


def _scores(q, k):
    return lax.dot_general(q, k, (((1,), (1,)), ((), ())), preferred_element_type=F32)


def _with_ones(v):
    return jnp.concatenate([v, jnp.ones_like(v)], axis=1)


def _softmax_pv(q, ks, vs):
    ss = [_scores(q, k) for k in ks]
    m = functools.reduce(jnp.maximum, [jnp.max(s, axis=-1, keepdims=True) for s in ss])
    o = functools.reduce(
        jnp.add,
        [jnp.dot(jnp.exp2(s - m).astype(BF), v, preferred_element_type=F32) for s, v in zip(ss, vs)])
    return o[:, :LANES] / o[:, LANES:LANES + 1]


def _kv_sources(keys, unit, k_ref, v_ref, cache_refs, cache_head):
    ks, vs = [k_ref[keys, unit]], [_with_ones(v_ref[keys, unit])]
    if cache_refs:
        kc_ref, vc_ref = cache_refs
        rows = pl.ds(cache_head, PAST_LEN, stride=kc_ref.shape[0] // PAST_LEN)
        ks.append(kc_ref[rows, :].astype(BF))
        vs.append(_with_ones(vc_ref[rows, :].astype(BF)))
    return ks, vs


def _gqa_kernel(seq, q_ref, k_ref, v_ref, *refs):
    *cache_refs, o_ref = refs
    n_kv = k_ref.shape[1] // HEAD_DIM
    grp = q_ref.shape[1] // HEAD_DIM // n_kv
    for s0 in range(0, q_ref.shape[0], seq):
        for kv in range(n_kv):
            ks, vs = _kv_sources(slice(s0, s0 + seq), slice(kv * HEAD_DIM, (kv + 1) * HEAD_DIM),
                                 k_ref, v_ref, cache_refs, pl.program_id(1) // (ATTN_Q_HEADS // ATTN_KV_HEADS))
            for h in range(kv * grp, (kv + 1) * grp):
                unit = slice(h * HEAD_DIM, (h + 1) * HEAD_DIM)
                for r in range(s0, s0 + seq, ATTN_ROWS):
                    rows = slice(r, r + ATTN_ROWS)
                    o_ref[rows, unit] = _softmax_pv(q_ref[rows, unit], ks, vs).astype(o_ref.dtype)


def _gqa(qa, ka, va, cache_k, cache_v, l, *, n_batch, seq):
    cached = cache_k is not None
    grp = ATTN_Q_HEADS // ATTN_KV_HEADS
    if cached:
        grid = (n_batch, ATTN_Q_HEADS)
        qspec = pl.BlockSpec((seq, HEAD_DIM), lambda b, h: (b, h))
        kvspec = pl.BlockSpec((seq, HEAD_DIM), lambda b, h: (b, h // grp))
        cspec = pl.BlockSpec((None, None, PAST_LEN * ATTN_KV_HEADS, HEAD_DIM), lambda b, h: (b, l, 0, 0))
        in_specs = [qspec, kvspec, kvspec, cspec, cspec]
        args = [qa, ka, va, cache_k, cache_v]
    else:
        grid = (n_batch // CTX_ATTN_SEQS, 1)
        qspec = pl.BlockSpec((CTX_ATTN_SEQS * seq, W_QA), lambda b, h: (b, 0))
        kvspec = pl.BlockSpec((CTX_ATTN_SEQS * seq, W_KA), lambda b, h: (b, 0))
        in_specs = [qspec, kvspec, kvspec]
        args = [qa, ka, va]
    return pl.pallas_call(
        functools.partial(_gqa_kernel, seq),
        grid=grid,
        in_specs=in_specs,
        out_specs=qspec,
        out_shape=jax.ShapeDtypeStruct((n_batch * seq, W_QA), BF),
        compiler_params=_cparams(2),
        name="gqa_lat" if cached else "gqa_ctx",
    )(*args)


def _diff_kernel(seq, lambda_init, lq1_ref, lk1_ref, lq2_ref, lk2_ref, sg_ref, q_ref, k_ref, v_ref, *refs):
    *cache_refs, o_ref = refs
    lam = (jnp.exp(jnp.sum(lq1_ref[...] * lk1_ref[...], axis=-1, keepdims=True))
           - jnp.exp(jnp.sum(lq2_ref[...] * lk2_ref[...], axis=-1, keepdims=True))
           + lambda_init)
    lo = lax.broadcasted_iota(jnp.int32, (1, LANES), 1) < DIFF_QK_DIM
    for s0 in range(0, q_ref.shape[0], seq):
        for h in range(q_ref.shape[1] // LANES):
            unit = slice(h * LANES, (h + 1) * LANES)
            ks, vs = _kv_sources(slice(s0, s0 + seq), unit, k_ref, v_ref, cache_refs, pl.program_id(1))
            for r in range(s0, s0 + seq, ATTN_ROWS):
                rows = slice(r, r + ATTN_ROWS)
                q = q_ref[rows, unit]
                zero = jnp.zeros_like(q)
                o = _softmax_pv(jnp.where(lo, q, zero), ks, vs) - lam * _softmax_pv(jnp.where(lo, zero, q), ks, vs)
                ms = jnp.mean(o * o, axis=-1, keepdims=True)
                y = (o * lax.rsqrt(ms + EPS)) * sg_ref[...]
                o_ref[rows, unit] = (y * (1.0 - lambda_init)).astype(o_ref.dtype)


def _diff(qb, kb, vb, cache_k, cache_v, lams, subln_g, l, lambda_init, *, n_batch, seq):
    cached = cache_k is not None
    lvec = pl.BlockSpec((None, 1, DIFF_QK_DIM), lambda b, h: (l, 0, 0))
    in_specs = [lvec] * 4 + [pl.BlockSpec((None, 1, DIFF_V_DIM), lambda b, h: (l, 0, 0))]
    args = list(lams) + [subln_g, qb, kb, vb]
    if cached:
        grid = (n_batch, DIFF_HEADS)
        spec = pl.BlockSpec((seq, LANES), lambda b, h: (b, h))
        cspec = pl.BlockSpec((None, None, PAST_LEN * DIFF_HEADS, LANES), lambda b, h: (b, l, 0, 0))
        in_specs += [spec] * 3 + [cspec] * 2
        args += [cache_k, cache_v]
    else:
        grid = (n_batch // CTX_ATTN_SEQS, 1)
        spec = pl.BlockSpec((CTX_ATTN_SEQS * seq, W_VB), lambda b, h: (b, 0))
        in_specs += [spec] * 3
    return pl.pallas_call(
        functools.partial(_diff_kernel, seq, lambda_init),
        grid=grid,
        in_specs=in_specs,
        out_specs=spec,
        out_shape=jax.ShapeDtypeStruct((n_batch * seq, W_VB), BF),
        compiler_params=_cparams(2),
        name="diff_lat" if cached else "diff_ctx",
    )(*args)


def _fourier_kernel(scale, f_ref, ct_ref, st_ref, cc_ref, sc_ref, w_ref, o_ref):
    f = f_ref[...]
    g1 = jnp.dot(f, cc_ref[...], preferred_element_type=F32).astype(BF)
    g2 = jnp.dot(f, sc_ref[...], preferred_element_type=F32).astype(BF)
    spec = (jnp.dot(ct_ref[...], g1, preferred_element_type=F32)
            - jnp.dot(st_ref[...], g2, preferred_element_type=F32)) * scale
    sb = spec.astype(BF)
    c = FOURIER_GROUP_DIM
    for g in range(FOURIER_GROUPS):
        o_ref[:, g * c:(g + 1) * c] = jnp.dot(
            sb[:, g * c:(g + 1) * c], w_ref[g], preferred_element_type=F32).astype(o_ref.dtype)


def _fourier(f, ct, st, cc, sc, w_fourier, l, *, n_batch, seq):
    scale = 1.0 / math.sqrt(seq * FOURIER_GROUP_DIM)
    c = FOURIER_GROUP_DIM
    return pl.pallas_call(
        functools.partial(_fourier_kernel, scale),
        grid=(n_batch,),
        in_specs=[
            pl.BlockSpec((seq, W_F), lambda b: (b, 0)),
            _resident((seq, seq), lambda b: (0, 0)),
            _resident((seq, seq), lambda b: (0, 0)),
            _resident((W_F, W_F), lambda b: (0, 0)),
            _resident((W_F, W_F), lambda b: (0, 0)),
            _resident((None, FOURIER_GROUPS, c, c), lambda b: (l, 0, 0, 0)),
        ],
        out_specs=pl.BlockSpec((seq, W_F), lambda b: (b, 0)),
        out_shape=jax.ShapeDtypeStruct((n_batch * seq, W_F), BF),
        compiler_params=_cparams(1),
        name="fourier_lat" if seq == DEC_SEQ else "fourier_ctx",
    )(f, ct, st, cc, sc, w_fourier)


def _dft_tables(n):
    k = np.arange(n, dtype=np.int64)
    ang = (2.0 * np.pi / n) * ((k[:, None] * k[None, :]) % n).astype(np.float64)
    return np.cos(ang), np.sin(ang)


def _oproj_kernel(x_ref, gt_ref, a_ref, b_ref, c_ref, w_ref, o_ref):
    acc = jnp.dot(a_ref[...], w_ref[0:W_QA, :], preferred_element_type=F32)
    acc += jnp.dot(b_ref[...], w_ref[W_QA:W_QA + W_VB, :], preferred_element_type=F32)
    acc += jnp.dot(c_ref[...], w_ref[W_QA + W_VB:MIX_WIDTH, :], preferred_element_type=F32)
    o_ref[...] = x_ref[...] + gt_ref[...] * acc


def _oproj(x, mods, l, out_a, out_b, out_c, w_out, *, latent, tm):
    n_tok = x.shape[0]
    per = DEC_SEQ // tm
    row_fn = (lambda i: i // per) if latent else (lambda i: CTX_ROW)
    return pl.pallas_call(
        _oproj_kernel,
        grid=(n_tok // tm,),
        in_specs=[
            pl.BlockSpec((tm, D_MODEL), lambda i: (i, 0)),
            _mod_spec(l, 2, row_fn),
            pl.BlockSpec((tm, W_QA), lambda i: (i, 0)),
            pl.BlockSpec((tm, W_VB), lambda i: (i, 0)),
            pl.BlockSpec((tm, W_F), lambda i: (i, 0)),
            _resident((None, MIX_WIDTH, D_MODEL), lambda i: (l, 0, 0)),
        ],
        out_specs=pl.BlockSpec((tm, D_MODEL), lambda i: (i, 0)),
        out_shape=jax.ShapeDtypeStruct((n_tok, D_MODEL), F32),
        compiler_params=_cparams(1),
        name="oproj_lat" if latent else "oproj_ctx",
    )(x, mods, out_a, out_b, out_c, w_out)


def _mlp_kernel(x_ref, sh_ref, sc_ref, gt_ref, g_ref, w1_ref, w2_ref, o_ref, h_scr):
    j = pl.program_id(1)

    @pl.when(j == 0)
    def _():
        _modulated_norm_store(x_ref, g_ref, sc_ref, sh_ref, h_scr)
        o_ref[...] = jnp.zeros_like(o_ref)

    hid = jnp.maximum(jnp.dot(h_scr[...], w1_ref[...], preferred_element_type=F32), 0.0)
    o_ref[...] += jnp.dot((hid * hid).astype(BF), w2_ref[...], preferred_element_type=F32)

    @pl.when(j == pl.num_programs(1) - 1)
    def _():
        o_ref[...] = x_ref[...] + gt_ref[...] * o_ref[...]


def _mlp(x, mods, l, norm_g, w1, w2, *, latent, tm, tf):
    n_tok = x.shape[0]
    per = DEC_SEQ // tm
    row_fn = (lambda i: i // per) if latent else (lambda i: CTX_ROW)
    return pl.pallas_call(
        _mlp_kernel,
        grid=(n_tok // tm, D_FF // tf),
        in_specs=[
            pl.BlockSpec((tm, D_MODEL), lambda i, j: (i, 0)),
            _mod_spec(l, 3, row_fn),
            _mod_spec(l, 4, row_fn),
            _mod_spec(l, 5, row_fn),
            pl.BlockSpec((None, 1, D_MODEL), lambda i, j: (l, 0, 0)),
            pl.BlockSpec((None, D_MODEL, tf), lambda i, j: (l, 0, j)),
            pl.BlockSpec((None, tf, D_MODEL), lambda i, j: (l, j, 0)),
        ],
        out_specs=pl.BlockSpec((tm, D_MODEL), lambda i, j: (i, 0)),
        out_shape=jax.ShapeDtypeStruct((n_tok, D_MODEL), F32),
        scratch_shapes=[pltpu.VMEM((tm, D_MODEL), BF)],
        compiler_params=_cparams(2),
        name="mlp_lat" if latent else "mlp_ctx",
    )(x, mods, mods, mods, norm_g, w1, w2)


def _rope_tables(n_tokens, rot_dim):
    rows = n_tokens // GRID_W
    row = jnp.repeat(jnp.arange(rows), GRID_W).astype(F32)
    col = jnp.tile(jnp.arange(GRID_W), rows).astype(F32)
    nf = rot_dim // 4
    inv = ROPE_THETA ** (-jnp.arange(nf, dtype=F32) / nf)
    ar, ac = row[:, None] * inv, col[:, None] * inv
    cos = jnp.concatenate([jnp.cos(ar), jnp.cos(ar), jnp.cos(ac), jnp.cos(ac)], axis=-1)
    sin = jnp.concatenate([-jnp.sin(ar), jnp.sin(ar), -jnp.sin(ac), jnp.sin(ac)], axis=-1)
    reps = LANES // rot_dim
    return jnp.tile(cos, (1, reps)), jnp.tile(sin, (1, reps))


def _block_diag(m, n):
    return np.kron(np.eye(n), m)


def kernel(x_prompt, x_sample, cache_attn_k, cache_attn_v, cache_diff_k, cache_diff_v, c, c_ctx, w_ada, b_ada, norm_mix_g, norm_mlp_g, w_in, q_norm_a, k_norm_a, q_norm_b, k_norm_b, lambda_q1, lambda_k1, lambda_q2, lambda_k2, subln_g, w_fourier, w_out, w_mlp_in, w_mlp_out):
    n_ctx, n_lat = BATCH * SEQ, DEC_BATCH * DEC_SEQ
    xp = x_prompt.reshape(n_ctx, D_MODEL)
    xs = x_sample.reshape(n_lat, D_MODEL)

    cond = jnp.zeros((N_COND, D_MODEL), F32).at[:DEC_BATCH].set(c).at[CTX_ROW].set(c_ctx)
    mods = _adaln_all(cond, w_ada, b_ada).reshape(DEPTH, N_COND, N_MOD, 1, D_MODEL)

    w_in_b = w_in.astype(BF)
    w_out_b = w_out.astype(BF)
    w1_b = w_mlp_in.astype(BF)
    w2_b = w_mlp_out.astype(BF)
    wf_b = w_fourier.astype(BF)

    row3 = lambda a: a.reshape(DEPTH, 1, a.shape[-1])
    norm_mix = row3(norm_mix_g)
    norm_mlp = row3(norm_mlp_g)
    qna, kna = row3(q_norm_a), row3(k_norm_a)
    qnb = row3(jnp.tile(q_norm_b, (1, LANES // DIFF_QK_DIM)))
    knb = row3(jnp.tile(k_norm_b, (1, LANES // DIFF_QK_DIM)))
    lams = [row3(a) for a in (lambda_q1, lambda_k1, lambda_q2, lambda_k2)]
    subg = row3(subln_g)

    tables = _rope_tables(DEC_SEQ, HEAD_DIM) + _rope_tables(DEC_SEQ, DIFF_QK_DIM)

    cch, sch = _dft_tables(FOURIER_GROUP_DIM)
    cc = jnp.asarray(_block_diag(cch, FOURIER_GROUPS), BF)
    sc = jnp.asarray(_block_diag(sch, FOURIER_GROUPS), BF)
    dft = {n: tuple(jnp.asarray(t, BF) for t in _dft_tables(n)) for n in (SEQ, DEC_SEQ)}

    ck_a = cache_attn_k.reshape(DEC_BATCH, DEPTH, PAST_LEN * ATTN_KV_HEADS, HEAD_DIM)
    cv_a = cache_attn_v.reshape(DEC_BATCH, DEPTH, PAST_LEN * ATTN_KV_HEADS, HEAD_DIM)
    ck_b = cache_diff_k.reshape(DEC_BATCH, DEPTH, PAST_LEN * DIFF_HEADS, 2 * DIFF_QK_DIM)
    cv_b = cache_diff_v.reshape(DEC_BATCH, DEPTH, PAST_LEN * DIFF_HEADS, DIFF_V_DIM)

    def layer(x, l, latent, states=None):
        lambda_init = 0.8 - 0.6 * math.exp(-0.3 * l)
        n_batch, seq = (DEC_BATCH, DEC_SEQ) if latent else (BATCH, SEQ)
        rows = _block_rows(latent)
        outs = _proj(x, mods, l, norm_mix, w_in_b, qna, kna, qnb, knb, tables if latent else states,
                     latent=latent, tm=rows["proj"])
        qa, ka, va, qb, kb, vb, f = outs[:7]
        caches = (ck_a, cv_a, ck_b, cv_b) if latent else (None,) * 4
        out_a = _gqa(qa, ka, va, caches[0], caches[1], l, n_batch=n_batch, seq=seq)
        out_b = _diff(qb, kb, vb, caches[2], caches[3], lams, subg, l, lambda_init, n_batch=n_batch, seq=seq)
        out_c = _fourier(f, dft[seq][0], dft[seq][1], cc, sc, wf_b, l, n_batch=n_batch, seq=seq)
        x = _oproj(x, mods, l, out_a, out_b, out_c, w_out_b, latent=latent, tm=rows["oproj"])
        x = _mlp(x, mods, l, norm_mlp, w1_b, w2_b, latent=latent, tm=rows["mlp"], tf=MLP_FF_BLOCK)
        return x, outs[7:]

    states = [jnp.zeros((BATCH, DEPTH, SEQ * n, LANES), F32)
              for n in (ATTN_KV_HEADS, ATTN_KV_HEADS, DIFF_HEADS, DIFF_HEADS)]
    for l in range(DEPTH):
        xp, states = layer(xp, l, False, states)
        xs, _ = layer(xs, l, True)

    return (
        xp.reshape(BATCH, SEQ, D_MODEL),
        xs.reshape(DEC_BATCH, DEC_SEQ, D_MODEL),
        states[0].reshape(BATCH, DEPTH, SEQ, ATTN_KV_HEADS, HEAD_DIM),
        states[1].reshape(BATCH, DEPTH, SEQ, ATTN_KV_HEADS, HEAD_DIM),
        states[2].reshape(BATCH, DEPTH, SEQ, DIFF_HEADS, 2, DIFF_QK_DIM),
        states[3].reshape(BATCH, DEPTH, SEQ, DIFF_HEADS, DIFF_V_DIM),
    )
```

```python
import functools
import math

import jax
import jax.numpy as jnp
import numpy as np
from jax import lax
from jax.experimental import pallas as pl
from jax.experimental.pallas import tpu as pltpu

D_MODEL = 2048
BATCH = 16
SEQ = 256
DEPTH = 4
DEC_BATCH = 8
DEC_SEQ = 2048
PAST_LEN = 512
GRID_W = 64
HEAD_DIM = 128
ATTN_Q_HEADS = 8
ATTN_KV_HEADS = 2
DIFF_HEADS = 4
DIFF_QK_DIM = 64
DIFF_V_DIM = 128
FOURIER_GROUPS = 4
FOURIER_GROUP_DIM = 128
D_FF = 4 * D_MODEL
ROPE_THETA = 10000.0
EPS = 1e-6
N_MOD = 6

W_QA = ATTN_Q_HEADS * HEAD_DIM
W_KA = ATTN_KV_HEADS * HEAD_DIM
W_VA = ATTN_KV_HEADS * HEAD_DIM
W_QB = DIFF_HEADS * 2 * DIFF_QK_DIM
W_KB = DIFF_HEADS * 2 * DIFF_QK_DIM
W_VB = DIFF_HEADS * DIFF_V_DIM
W_F = FOURIER_GROUPS * FOURIER_GROUP_DIM
IN_WIDTH = W_QA + W_KA + W_VA + W_QB + W_KB + W_VB + W_F
MIX_WIDTH = W_QA + W_VB + W_F

LANES = 128
NORM_ROWS = 16
ATTN_ROWS = 128
CTX_ATTN_SEQS = 4
LAT_GQA_HEADS = 2
LOG2E = math.log2(math.e)
N_COND = 16
CTX_ROW = DEC_BATCH
VMEM_LIMIT_BYTES = 56 * 1024 * 1024

BF = jnp.bfloat16
F32 = jnp.float32


MLP_FF_BLOCK = 1024


def _block_rows(latent):
    if latent:
        return {"proj": 512, "oproj": 512, "mlp": 512}
    return {"proj": SEQ, "oproj": 512, "mlp": 512}


def _cparams(n_axes):
    return pltpu.CompilerParams(
        dimension_semantics=("arbitrary",) * n_axes, vmem_limit_bytes=VMEM_LIMIT_BYTES
    )


def _resident(block_shape, index_map):
    return pl.BlockSpec(block_shape, index_map, pipeline_mode=pl.Buffered(1))


def _adaln_kernel(c_ref, w_ref, b_ref, o_ref):
    c = c_ref[...]
    s = (c * jax.nn.sigmoid(c)).astype(BF)
    o_ref[...] = jnp.dot(s, w_ref[...].astype(BF), preferred_element_type=F32) + b_ref[...]


def _adaln_all(cond, w_ada, b_ada):
    tn = 1024
    n_out = N_MOD * D_MODEL
    return pl.pallas_call(
        _adaln_kernel,
        grid=(DEPTH, n_out // tn),
        in_specs=[
            pl.BlockSpec((N_COND, D_MODEL), lambda l, j: (0, 0)),
            pl.BlockSpec((None, D_MODEL, tn), lambda l, j: (l, 0, j)),
            pl.BlockSpec((None, 1, tn), lambda l, j: (l, 0, j)),
        ],
        out_specs=pl.BlockSpec((None, N_COND, tn), lambda l, j: (l, 0, j)),
        out_shape=jax.ShapeDtypeStruct((DEPTH, N_COND, n_out), F32),
        compiler_params=_cparams(2),
        name="adaln",
    )(cond, w_ada, b_ada.reshape(DEPTH, 1, n_out))


def _mod_spec(l, k, row_fn):
    return pl.BlockSpec((None, None, None, 1, D_MODEL), lambda i, *_: (l, row_fn(i), k, 0, 0))


def _modulated_norm(x, g, scale, shift):
    ms = jnp.mean(x * x, axis=-1, keepdims=True)
    h = (x * lax.rsqrt(ms + EPS)) * g
    return (h * (1.0 + scale) + shift).astype(BF)


def _modulated_norm_store(x_ref, g_ref, sc_ref, sh_ref, h_ref):
    def body(r, carry):
        rows = pl.ds(pl.multiple_of(r * NORM_ROWS, NORM_ROWS), NORM_ROWS)
        h_ref[rows, :] = _modulated_norm(x_ref[rows, :], g_ref[...], sc_ref[...], sh_ref[...])
        return carry

    lax.fori_loop(0, x_ref.shape[0] // NORM_ROWS, body, 0, unroll=4)


def _group_rms(z, gain, group):
    zz = z * z
    if group == LANES:
        ms = jnp.sum(zz, axis=-1, keepdims=True) * (1.0 / LANES)
    else:
        lo = lax.broadcasted_iota(jnp.int32, (1, LANES), 1) < group
        a = jnp.sum(jnp.where(lo, zz, 0.0), axis=-1, keepdims=True)
        b = jnp.sum(jnp.where(lo, 0.0, zz), axis=-1, keepdims=True)
        ms = jnp.where(lo, a, b) * (1.0 / group)
    return (z * lax.rsqrt(ms + EPS)) * gain


def _rope(y, cos, sin, blk):
    lane = lax.broadcasted_iota(jnp.int32, (1, LANES), 1)
    first = (lane & (2 * blk - 1)) < blk
    partner = jnp.where(first, pltpu.roll(y, LANES - blk, 1), pltpu.roll(y, blk, 1))
    return y * cos + partner * sin


def _proj_kernel(latent, *refs):
    x_ref, sh_ref, sc_ref, g_ref, w_ref, qna_ref, kna_ref, qnb_ref, knb_ref = refs[:9]
    refs = refs[9:]
    if latent:
        cosa_ref, sina_ref, cosb_ref, sinb_ref = refs[:4]
        refs = refs[4:]
    qa_ref, ka_ref, va_ref, qb_ref, kb_ref, vb_ref, f_ref = refs[:7]
    if not latent:
        ska_ref, sva_ref, skb_ref, svb_ref = refs[7:]

    hb = _modulated_norm(x_ref[...], g_ref[...], sc_ref[...], sh_ref[...])

    def cols(c0, width):
        return jnp.dot(hb, w_ref[:, c0:c0 + width], preferred_element_type=F32)

    def rope_a(y):
        return _rope(y, cosa_ref[...], sina_ref[...], HEAD_DIM // 4) if latent else y

    def rope_b(y):
        return _rope(y, cosb_ref[...], sinb_ref[...], DIFF_QK_DIM // 4) if latent else y

    def unit(z, u):
        return z[:, u * LANES:(u + 1) * LANES]

    def put(ref, u, val):
        ref[:, u * LANES:(u + 1) * LANES] = val.astype(ref.dtype)

    c_qa, c_ka = 0, W_QA
    c_qb = c_ka + W_KA + W_VA
    c_kb = c_qb + W_QB
    c_vb = c_kb + W_KB
    c_f = c_vb + W_VB

    z = cols(c_qb, W_QB)
    for u in range(DIFF_HEADS):
        y = _group_rms(unit(z, u), qnb_ref[...], DIFF_QK_DIM)
        put(qb_ref, u, rope_b(y) * (DIFF_QK_DIM ** -0.5 * LOG2E))

    z = cols(c_kb, W_KB)
    for u in range(DIFF_HEADS):
        y = _group_rms(unit(z, u), knb_ref[...], DIFF_QK_DIM)
        if not latent:
            put(skb_ref, u, y)
        put(kb_ref, u, rope_b(y))

    chunk = 4 * LANES
    for half in range(W_QA // chunk):
        z = cols(c_qa + half * chunk, chunk)
        for u in range(4):
            y = _group_rms(unit(z, u), qna_ref[...], HEAD_DIM)
            put(qa_ref, half * 4 + u, rope_a(y) * (HEAD_DIM ** -0.5 * LOG2E))

    z = cols(c_ka, W_KA + W_VA)
    for u in range(ATTN_KV_HEADS):
        y = _group_rms(unit(z, u), kna_ref[...], HEAD_DIM)
        if not latent:
            put(ska_ref, u, y)
        put(ka_ref, u, rope_a(y))
        v = unit(z, ATTN_KV_HEADS + u)
        if not latent:
            put(sva_ref, u, v)
        put(va_ref, u, v)

    z = cols(c_vb, W_VB)
    if not latent:
        svb_ref[...] = z
    vb_ref[...] = z.astype(BF)

    f_ref[...] = cols(c_f, W_F).astype(BF)


def _proj(x, mods, l, norm_g, w_in, qna, kna, qnb, knb, tables, *, latent, tm):
    n_tok = x.shape[0]
    n_blk = n_tok // tm
    per = DEC_SEQ // tm
    row_fn = (lambda i: i // per) if latent else (lambda i: CTX_ROW)
    vec = lambda n: pl.BlockSpec((None, 1, n), lambda i: (l, 0, 0))
    in_specs = [
        pl.BlockSpec((tm, D_MODEL), lambda i: (i, 0)),
        _mod_spec(l, 0, row_fn),
        _mod_spec(l, 1, row_fn),
        vec(D_MODEL),
        _resident((None, D_MODEL, IN_WIDTH), lambda i: (l, 0, 0)),
        vec(LANES), vec(LANES), vec(LANES), vec(LANES),
    ]
    args = [x, mods, mods, norm_g, w_in, qna, kna, qnb, knb]
    widths = [W_QA, W_KA, W_VA, W_QB, W_KB, W_VB, W_F]
    out_specs = [pl.BlockSpec((tm, w), lambda i: (i, 0)) for w in widths]
    out_shape = [jax.ShapeDtypeStruct((n_tok, w), BF) for w in widths]
    if latent:
        in_specs += [pl.BlockSpec((tm, LANES), lambda i: (i % per, 0))] * 4
        args += list(tables)
    else:
        swidths = [W_KA, W_VA, W_KB, W_VB]
        out_specs += [pl.BlockSpec((tm, w), lambda i: (i, 0)) for w in swidths]
        out_shape += [jax.ShapeDtypeStruct((n_tok, w), F32) for w in swidths]
    return pl.pallas_call(
        functools.partial(_proj_kernel, latent),
        grid=(n_blk,),
        in_specs=in_specs,
        out_specs=out_specs,
        out_shape=out_shape,
        compiler_params=_cparams(1),
        name="proj_lat" if latent else "proj_ctx",
    )(*args)


def _scores(q, k):
    return lax.dot_general(q, k, (((1,), (1,)), ((), ())), preferred_element_type=F32)


def _with_ones(v):
    return jnp.concatenate([v, jnp.ones_like(v)], axis=1)


def _softmax_pv(q, ks, vs):
    ss = [_scores(q, k) for k in ks]
    m = functools.reduce(jnp.maximum, [jnp.max(s, axis=-1, keepdims=True) for s in ss])
    o = functools.reduce(
        jnp.add,
        [jnp.dot(jnp.exp2(s - m).astype(BF), v, preferred_element_type=F32) for s, v in zip(ss, vs)])
    return o[:, :LANES] / o[:, LANES:LANES + 1]


def _kv_sources(keys, unit, k_ref, v_ref, cache_refs):
    ks, vs = [k_ref[keys, unit]], [_with_ones(v_ref[keys, unit])]
    if cache_refs:
        kc_ref, vc_ref = cache_refs
        ks.append(kc_ref[:, unit].astype(BF))
        vs.append(_with_ones(vc_ref[:, unit].astype(BF)))
    return ks, vs


def _gqa_kernel(seq, q_ref, k_ref, v_ref, *refs):
    *cache_refs, o_ref = refs
    n_kv = k_ref.shape[1] // HEAD_DIM
    grp = q_ref.shape[1] // HEAD_DIM // n_kv
    for s0 in range(0, q_ref.shape[0], seq):
        for kv in range(n_kv):
            ks, vs = _kv_sources(slice(s0, s0 + seq), slice(kv * HEAD_DIM, (kv + 1) * HEAD_DIM),
                                 k_ref, v_ref, cache_refs)
            for h in range(kv * grp, (kv + 1) * grp):
                unit = slice(h * HEAD_DIM, (h + 1) * HEAD_DIM)
                for r in range(s0, s0 + seq, ATTN_ROWS):
                    rows = slice(r, r + ATTN_ROWS)
                    o_ref[rows, unit] = _softmax_pv(q_ref[rows, unit], ks, vs).astype(o_ref.dtype)


def _gqa(qa, ka, va, cache_k, cache_v, l, *, n_batch, seq):
    cached = cache_k is not None
    grp = ATTN_Q_HEADS // ATTN_KV_HEADS
    if cached:
        n_h = LAT_GQA_HEADS
        grid = (n_batch, ATTN_Q_HEADS // n_h)
        qspec = pl.BlockSpec((seq, n_h * HEAD_DIM), lambda b, h: (b, h))
        kvspec = pl.BlockSpec((seq, HEAD_DIM), lambda b, h: (b, h * n_h // grp))
        cspec = pl.BlockSpec((None, None, PAST_LEN, HEAD_DIM), lambda b, h: (b, l, 0, h * n_h // grp))
        in_specs = [qspec, kvspec, kvspec, cspec, cspec]
        args = [qa, ka, va, cache_k, cache_v]
    else:
        grid = (n_batch // CTX_ATTN_SEQS, 1)
        qspec = pl.BlockSpec((CTX_ATTN_SEQS * seq, W_QA), lambda b, h: (b, 0))
        kvspec = pl.BlockSpec((CTX_ATTN_SEQS * seq, W_KA), lambda b, h: (b, 0))
        in_specs = [qspec, kvspec, kvspec]
        args = [qa, ka, va]
    return pl.pallas_call(
        functools.partial(_gqa_kernel, seq),
        grid=grid,
        in_specs=in_specs,
        out_specs=qspec,
        out_shape=jax.ShapeDtypeStruct((n_batch * seq, W_QA), BF),
        compiler_params=_cparams(2),
        name="gqa_lat" if cached else "gqa_ctx",
    )(*args)


def _diff_kernel(seq, lambda_init, lq1_ref, lk1_ref, lq2_ref, lk2_ref, sg_ref, q_ref, k_ref, v_ref, *refs):
    *cache_refs, o_ref = refs
    lam = (jnp.exp(jnp.sum(lq1_ref[...] * lk1_ref[...], axis=-1, keepdims=True))
           - jnp.exp(jnp.sum(lq2_ref[...] * lk2_ref[...], axis=-1, keepdims=True))
           + lambda_init)
    lo = lax.broadcasted_iota(jnp.int32, (1, LANES), 1) < DIFF_QK_DIM
    for s0 in range(0, q_ref.shape[0], seq):
        for h in range(q_ref.shape[1] // LANES):
            unit = slice(h * LANES, (h + 1) * LANES)
            ks, vs = _kv_sources(slice(s0, s0 + seq), unit, k_ref, v_ref, cache_refs)
            for r in range(s0, s0 + seq, ATTN_ROWS):
                rows = slice(r, r + ATTN_ROWS)
                q = q_ref[rows, unit]
                zero = jnp.zeros_like(q)
                o = _softmax_pv(jnp.where(lo, q, zero), ks, vs) - lam * _softmax_pv(jnp.where(lo, zero, q), ks, vs)
                ms = jnp.mean(o * o, axis=-1, keepdims=True)
                y = (o * lax.rsqrt(ms + EPS)) * sg_ref[...]
                o_ref[rows, unit] = (y * (1.0 - lambda_init)).astype(o_ref.dtype)


def _diff(qb, kb, vb, cache_k, cache_v, lams, subln_g, l, lambda_init, *, n_batch, seq):
    cached = cache_k is not None
    lvec = pl.BlockSpec((None, 1, DIFF_QK_DIM), lambda b, h: (l, 0, 0))
    in_specs = [lvec] * 4 + [pl.BlockSpec((None, 1, DIFF_V_DIM), lambda b, h: (l, 0, 0))]
    args = list(lams) + [subln_g, qb, kb, vb]
    if cached:
        grid = (n_batch, DIFF_HEADS)
        spec = pl.BlockSpec((seq, LANES), lambda b, h: (b, h))
        cspec = pl.BlockSpec((None, None, PAST_LEN, LANES), lambda b, h: (b, l, 0, h))
        in_specs += [spec] * 3 + [cspec] * 2
        args += [cache_k, cache_v]
    else:
        grid = (n_batch // CTX_ATTN_SEQS, 1)
        spec = pl.BlockSpec((CTX_ATTN_SEQS * seq, W_VB), lambda b, h: (b, 0))
        in_specs += [spec] * 3
    return pl.pallas_call(
        functools.partial(_diff_kernel, seq, lambda_init),
        grid=grid,
        in_specs=in_specs,
        out_specs=spec,
        out_shape=jax.ShapeDtypeStruct((n_batch * seq, W_VB), BF),
        compiler_params=_cparams(2),
        name="diff_lat" if cached else "diff_ctx",
    )(*args)


def _fourier_kernel(scale, f_ref, ct_ref, st_ref, cc_ref, sc_ref, w_ref, o_ref):
    f = f_ref[...]
    g1 = jnp.dot(f, cc_ref[...], preferred_element_type=F32).astype(BF)
    g2 = jnp.dot(f, sc_ref[...], preferred_element_type=F32).astype(BF)
    spec = (jnp.dot(ct_ref[...], g1, preferred_element_type=F32)
            - jnp.dot(st_ref[...], g2, preferred_element_type=F32)) * scale
    sb = spec.astype(BF)
    c = FOURIER_GROUP_DIM
    for g in range(FOURIER_GROUPS):
        o_ref[:, g * c:(g + 1) * c] = jnp.dot(
            sb[:, g * c:(g + 1) * c], w_ref[g], preferred_element_type=F32).astype(o_ref.dtype)


def _fourier(f, ct, st, cc, sc, w_fourier, l, *, n_batch, seq):
    scale = 1.0 / math.sqrt(seq * FOURIER_GROUP_DIM)
    c = FOURIER_GROUP_DIM
    return pl.pallas_call(
        functools.partial(_fourier_kernel, scale),
        grid=(n_batch,),
        in_specs=[
            pl.BlockSpec((seq, W_F), lambda b: (b, 0)),
            _resident((seq, seq), lambda b: (0, 0)),
            _resident((seq, seq), lambda b: (0, 0)),
            _resident((W_F, W_F), lambda b: (0, 0)),
            _resident((W_F, W_F), lambda b: (0, 0)),
            _resident((None, FOURIER_GROUPS, c, c), lambda b: (l, 0, 0, 0)),
        ],
        out_specs=pl.BlockSpec((seq, W_F), lambda b: (b, 0)),
        out_shape=jax.ShapeDtypeStruct((n_batch * seq, W_F), BF),
        compiler_params=_cparams(1),
        name="fourier_lat" if seq == DEC_SEQ else "fourier_ctx",
    )(f, ct, st, cc, sc, w_fourier)


def _dft_tables(n):
    k = np.arange(n, dtype=np.int64)
    ang = (2.0 * np.pi / n) * ((k[:, None] * k[None, :]) % n).astype(np.float64)
    return np.cos(ang), np.sin(ang)


def _oproj_kernel(x_ref, gt_ref, a_ref, b_ref, c_ref, w_ref, o_ref):
    acc = jnp.dot(a_ref[...], w_ref[0:W_QA, :], preferred_element_type=F32)
    acc += jnp.dot(b_ref[...], w_ref[W_QA:W_QA + W_VB, :], preferred_element_type=F32)
    acc += jnp.dot(c_ref[...], w_ref[W_QA + W_VB:MIX_WIDTH, :], preferred_element_type=F32)
    o_ref[...] = x_ref[...] + gt_ref[...] * acc


def _oproj(x, mods, l, out_a, out_b, out_c, w_out, *, latent, tm):
    n_tok = x.shape[0]
    per = DEC_SEQ // tm
    row_fn = (lambda i: i // per) if latent else (lambda i: CTX_ROW)
    return pl.pallas_call(
        _oproj_kernel,
        grid=(n_tok // tm,),
        in_specs=[
            pl.BlockSpec((tm, D_MODEL), lambda i: (i, 0)),
            _mod_spec(l, 2, row_fn),
            pl.BlockSpec((tm, W_QA), lambda i: (i, 0)),
            pl.BlockSpec((tm, W_VB), lambda i: (i, 0)),
            pl.BlockSpec((tm, W_F), lambda i: (i, 0)),
            _resident((None, MIX_WIDTH, D_MODEL), lambda i: (l, 0, 0)),
        ],
        out_specs=pl.BlockSpec((tm, D_MODEL), lambda i: (i, 0)),
        out_shape=jax.ShapeDtypeStruct((n_tok, D_MODEL), F32),
        compiler_params=_cparams(1),
        name="oproj_lat" if latent else "oproj_ctx",
    )(x, mods, out_a, out_b, out_c, w_out)


def _mlp_kernel(x_ref, sh_ref, sc_ref, gt_ref, g_ref, w1_ref, w2_ref, o_ref, h_scr):
    j = pl.program_id(1)

    @pl.when(j == 0)
    def _():
        _modulated_norm_store(x_ref, g_ref, sc_ref, sh_ref, h_scr)
        o_ref[...] = jnp.zeros_like(o_ref)

    hid = jnp.maximum(jnp.dot(h_scr[...], w1_ref[...], preferred_element_type=F32), 0.0)
    o_ref[...] += jnp.dot((hid * hid).astype(BF), w2_ref[...], preferred_element_type=F32)

    @pl.when(j == pl.num_programs(1) - 1)
    def _():
        o_ref[...] = x_ref[...] + gt_ref[...] * o_ref[...]


def _mlp(x, mods, l, norm_g, w1, w2, *, latent, tm, tf):
    n_tok = x.shape[0]
    per = DEC_SEQ // tm
    row_fn = (lambda i: i // per) if latent else (lambda i: CTX_ROW)
    return pl.pallas_call(
        _mlp_kernel,
        grid=(n_tok // tm, D_FF // tf),
        in_specs=[
            pl.BlockSpec((tm, D_MODEL), lambda i, j: (i, 0)),
            _mod_spec(l, 3, row_fn),
            _mod_spec(l, 4, row_fn),
            _mod_spec(l, 5, row_fn),
            pl.BlockSpec((None, 1, D_MODEL), lambda i, j: (l, 0, 0)),
            pl.BlockSpec((None, D_MODEL, tf), lambda i, j: (l, 0, j)),
            pl.BlockSpec((None, tf, D_MODEL), lambda i, j: (l, j, 0)),
        ],
        out_specs=pl.BlockSpec((tm, D_MODEL), lambda i, j: (i, 0)),
        out_shape=jax.ShapeDtypeStruct((n_tok, D_MODEL), F32),
        scratch_shapes=[pltpu.VMEM((tm, D_MODEL), BF)],
        compiler_params=_cparams(2),
        name="mlp_lat" if latent else "mlp_ctx",
    )(x, mods, mods, mods, norm_g, w1, w2)


def _rope_tables(n_tokens, rot_dim):
    rows = n_tokens // GRID_W
    row = jnp.repeat(jnp.arange(rows), GRID_W).astype(F32)
    col = jnp.tile(jnp.arange(GRID_W), rows).astype(F32)
    nf = rot_dim // 4
    inv = ROPE_THETA ** (-jnp.arange(nf, dtype=F32) / nf)
    ar, ac = row[:, None] * inv, col[:, None] * inv
    cos = jnp.concatenate([jnp.cos(ar), jnp.cos(ar), jnp.cos(ac), jnp.cos(ac)], axis=-1)
    sin = jnp.concatenate([-jnp.sin(ar), jnp.sin(ar), -jnp.sin(ac), jnp.sin(ac)], axis=-1)
    reps = LANES // rot_dim
    return jnp.tile(cos, (1, reps)), jnp.tile(sin, (1, reps))


def _block_diag(m, n):
    return np.kron(np.eye(n), m)


def kernel(x_prompt, x_sample, cache_attn_k, cache_attn_v, cache_diff_k, cache_diff_v, c, c_ctx, w_ada, b_ada, norm_mix_g, norm_mlp_g, w_in, q_norm_a, k_norm_a, q_norm_b, k_norm_b, lambda_q1, lambda_k1, lambda_q2, lambda_k2, subln_g, w_fourier, w_out, w_mlp_in, w_mlp_out):
    n_ctx, n_lat = BATCH * SEQ, DEC_BATCH * DEC_SEQ
    xp = x_prompt.reshape(n_ctx, D_MODEL)
    xs = x_sample.reshape(n_lat, D_MODEL)

    cond = jnp.zeros((N_COND, D_MODEL), F32).at[:DEC_BATCH].set(c).at[CTX_ROW].set(c_ctx)
    mods = _adaln_all(cond, w_ada, b_ada).reshape(DEPTH, N_COND, N_MOD, 1, D_MODEL)

    w_in_b = w_in.astype(BF)
    w_out_b = w_out.astype(BF)
    w1_b = w_mlp_in.astype(BF)
    w2_b = w_mlp_out.astype(BF)
    wf_b = w_fourier.astype(BF)

    row3 = lambda a: a.reshape(DEPTH, 1, a.shape[-1])
    norm_mix = row3(norm_mix_g)
    norm_mlp = row3(norm_mlp_g)
    qna, kna = row3(q_norm_a), row3(k_norm_a)
    qnb = row3(jnp.tile(q_norm_b, (1, LANES // DIFF_QK_DIM)))
    knb = row3(jnp.tile(k_norm_b, (1, LANES // DIFF_QK_DIM)))
    lams = [row3(a) for a in (lambda_q1, lambda_k1, lambda_q2, lambda_k2)]
    subg = row3(subln_g)

    tables = _rope_tables(DEC_SEQ, HEAD_DIM) + _rope_tables(DEC_SEQ, DIFF_QK_DIM)

    cch, sch = _dft_tables(FOURIER_GROUP_DIM)
    cc = jnp.asarray(_block_diag(cch, FOURIER_GROUPS), BF)
    sc = jnp.asarray(_block_diag(sch, FOURIER_GROUPS), BF)
    dft = {n: tuple(jnp.asarray(t, BF) for t in _dft_tables(n)) for n in (SEQ, DEC_SEQ)}

    ck_a = cache_attn_k.reshape(DEC_BATCH, DEPTH, PAST_LEN, W_KA)
    cv_a = cache_attn_v.reshape(DEC_BATCH, DEPTH, PAST_LEN, W_VA)
    ck_b = cache_diff_k.reshape(DEC_BATCH, DEPTH, PAST_LEN, W_KB)
    cv_b = cache_diff_v.reshape(DEC_BATCH, DEPTH, PAST_LEN, W_VB)

    def layer(x, l, latent):
        lambda_init = 0.8 - 0.6 * math.exp(-0.3 * l)
        n_batch, seq = (DEC_BATCH, DEC_SEQ) if latent else (BATCH, SEQ)
        rows = _block_rows(latent)
        outs = _proj(x, mods, l, norm_mix, w_in_b, qna, kna, qnb, knb, tables, latent=latent, tm=rows["proj"])
        qa, ka, va, qb, kb, vb, f = outs[:7]
        caches = (ck_a, cv_a, ck_b, cv_b) if latent else (None,) * 4
        out_a = _gqa(qa, ka, va, caches[0], caches[1], l, n_batch=n_batch, seq=seq)
        out_b = _diff(qb, kb, vb, caches[2], caches[3], lams, subg, l, lambda_init, n_batch=n_batch, seq=seq)
        out_c = _fourier(f, dft[seq][0], dft[seq][1], cc, sc, wf_b, l, n_batch=n_batch, seq=seq)
        x = _oproj(x, mods, l, out_a, out_b, out_c, w_out_b, latent=latent, tm=rows["oproj"])
        x = _mlp(x, mods, l, norm_mlp, w1_b, w2_b, latent=latent, tm=rows["mlp"], tf=MLP_FF_BLOCK)
        return x, outs[7:]

    states = []
    for l in range(DEPTH):
        xp, st = layer(xp, l, False)
        states.append(st)
        xs, _ = layer(xs, l, True)

    def stack(k, shape):
        return jnp.stack([s[k].reshape((BATCH, SEQ) + shape) for s in states], axis=1)

    return (
        xp.reshape(BATCH, SEQ, D_MODEL),
        xs.reshape(DEC_BATCH, DEC_SEQ, D_MODEL),
        stack(0, (ATTN_KV_HEADS, HEAD_DIM)),
        stack(1, (ATTN_KV_HEADS, HEAD_DIM)),
        stack(2, (DIFF_HEADS, 2, DIFF_QK_DIM)),
        stack(3, (DIFF_HEADS, DIFF_V_DIM)),
    )
```

```python
import functools
import math

import jax
import jax.numpy as jnp
import numpy as np
from jax import lax
from jax.experimental import pallas as pl
from jax.experimental.pallas import tpu as pltpu

D_MODEL = 2048
BATCH = 16
SEQ = 256
DEPTH = 4
DEC_BATCH = 8
DEC_SEQ = 2048
PAST_LEN = 512
GRID_W = 64
HEAD_DIM = 128
ATTN_Q_HEADS = 8
ATTN_KV_HEADS = 2
DIFF_HEADS = 4
DIFF_QK_DIM = 64
DIFF_V_DIM = 128
FOURIER_GROUPS = 4
FOURIER_GROUP_DIM = 128
D_FF = 4 * D_MODEL
ROPE_THETA = 10000.0
EPS = 1e-6
N_MOD = 6

W_QA = ATTN_Q_HEADS * HEAD_DIM
W_KA = ATTN_KV_HEADS * HEAD_DIM
W_VA = ATTN_KV_HEADS * HEAD_DIM
W_QB = DIFF_HEADS * 2 * DIFF_QK_DIM
W_KB = DIFF_HEADS * 2 * DIFF_QK_DIM
W_VB = DIFF_HEADS * DIFF_V_DIM
W_F = FOURIER_GROUPS * FOURIER_GROUP_DIM
IN_WIDTH = W_QA + W_KA + W_VA + W_QB + W_KB + W_VB + W_F
MIX_WIDTH = W_QA + W_VB + W_F

LANES = 128
NORM_ROWS = 16
ATTN_ROWS = 128
DIFF_LAT_ROWS = 256
CTX_ATTN_SEQS = 4
LAT_GQA_HEADS = 2
LOG2E = math.log2(math.e)
N_COND = 16
CTX_ROW = DEC_BATCH
VMEM_LIMIT_BYTES = 56 * 1024 * 1024

BF = jnp.bfloat16
F32 = jnp.float32


MLP_FF_BLOCK = 1024


def _block_rows(latent):
    if latent:
        return {"proj": 512, "oproj": 512, "mlp": 512}
    return {"proj": SEQ, "oproj": 512, "mlp": 512}


def _cparams(n_axes):
    return pltpu.CompilerParams(
        dimension_semantics=("arbitrary",) * n_axes, vmem_limit_bytes=VMEM_LIMIT_BYTES
    )


def _resident(block_shape, index_map):
    return pl.BlockSpec(block_shape, index_map, pipeline_mode=pl.Buffered(1))


def _adaln_kernel(c_ref, w_ref, b_ref, o_ref):
    c = c_ref[...]
    s = (c * jax.nn.sigmoid(c)).astype(BF)
    o_ref[...] = jnp.dot(s, w_ref[...].astype(BF), preferred_element_type=F32) + b_ref[...]


def _adaln_all(cond, w_ada, b_ada):
    tn = 1024
    n_out = N_MOD * D_MODEL
    return pl.pallas_call(
        _adaln_kernel,
        grid=(DEPTH, n_out // tn),
        in_specs=[
            pl.BlockSpec((N_COND, D_MODEL), lambda l, j: (0, 0)),
            pl.BlockSpec((None, D_MODEL, tn), lambda l, j: (l, 0, j)),
            pl.BlockSpec((None, 1, tn), lambda l, j: (l, 0, j)),
        ],
        out_specs=pl.BlockSpec((None, N_COND, tn), lambda l, j: (l, 0, j)),
        out_shape=jax.ShapeDtypeStruct((DEPTH, N_COND, n_out), F32),
        compiler_params=_cparams(2),
        name="adaln",
    )(cond, w_ada, b_ada.reshape(DEPTH, 1, n_out))


def _mod_spec(l, k, row_fn):
    return pl.BlockSpec((None, None, None, 1, D_MODEL), lambda i, *_: (l, row_fn(i), k, 0, 0))


def _modulated_norm(x, g, scale, shift):
    ms = jnp.mean(x * x, axis=-1, keepdims=True)
    h = (x * lax.rsqrt(ms + EPS)) * g
    return (h * (1.0 + scale) + shift).astype(BF)


def _modulated_norm_store(x_ref, g_ref, sc_ref, sh_ref, h_ref):
    def body(r, carry):
        rows = pl.ds(pl.multiple_of(r * NORM_ROWS, NORM_ROWS), NORM_ROWS)
        h_ref[rows, :] = _modulated_norm(x_ref[rows, :], g_ref[...], sc_ref[...], sh_ref[...])
        return carry

    lax.fori_loop(0, x_ref.shape[0] // NORM_ROWS, body, 0, unroll=4)


def _group_rms(z, gain, group):
    zz = z * z
    if group == LANES:
        ms = jnp.sum(zz, axis=-1, keepdims=True) * (1.0 / LANES)
    else:
        lo = lax.broadcasted_iota(jnp.int32, (1, LANES), 1) < group
        a = jnp.sum(jnp.where(lo, zz, 0.0), axis=-1, keepdims=True)
        b = jnp.sum(jnp.where(lo, 0.0, zz), axis=-1, keepdims=True)
        ms = jnp.where(lo, a, b) * (1.0 / group)
    return (z * lax.rsqrt(ms + EPS)) * gain


def _rope(y, cos, sin, blk):
    lane = lax.broadcasted_iota(jnp.int32, (1, LANES), 1)
    first = (lane & (2 * blk - 1)) < blk
    partner = jnp.where(first, pltpu.roll(y, LANES - blk, 1), pltpu.roll(y, blk, 1))
    return y * cos + partner * sin


def _proj_kernel(latent, *refs):
    x_ref, sh_ref, sc_ref, g_ref, w_ref, qna_ref, kna_ref, qnb_ref, knb_ref = refs[:9]
    refs = refs[9:]
    if latent:
        cosa_ref, sina_ref, cosb_ref, sinb_ref = refs[:4]
        refs = refs[4:]
    qa_ref, ka_ref, va_ref, qb_ref, kb_ref, vb_ref, f_ref = refs[:7]
    if not latent:
        ska_ref, sva_ref, skb_ref, svb_ref = refs[7:]

    hb = _modulated_norm(x_ref[...], g_ref[...], sc_ref[...], sh_ref[...])

    def cols(c0, width):
        return jnp.dot(hb, w_ref[:, c0:c0 + width], preferred_element_type=F32)

    def rope_a(y):
        return _rope(y, cosa_ref[...], sina_ref[...], HEAD_DIM // 4) if latent else y

    def rope_b(y):
        return _rope(y, cosb_ref[...], sinb_ref[...], DIFF_QK_DIM // 4) if latent else y

    def unit(z, u):
        return z[:, u * LANES:(u + 1) * LANES]

    def put(ref, u, val):
        ref[:, u * LANES:(u + 1) * LANES] = val.astype(ref.dtype)

    c_qa, c_ka = 0, W_QA
    c_qb = c_ka + W_KA + W_VA
    c_kb = c_qb + W_QB
    c_vb = c_kb + W_KB
    c_f = c_vb + W_VB

    z = cols(c_qb, W_QB)
    for u in range(DIFF_HEADS):
        y = _group_rms(unit(z, u), qnb_ref[...], DIFF_QK_DIM)
        put(qb_ref, u, rope_b(y) * (DIFF_QK_DIM ** -0.5 * LOG2E))

    z = cols(c_kb, W_KB)
    for u in range(DIFF_HEADS):
        y = _group_rms(unit(z, u), knb_ref[...], DIFF_QK_DIM)
        if not latent:
            put(skb_ref, u, y)
        put(kb_ref, u, rope_b(y))

    chunk = 4 * LANES
    for half in range(W_QA // chunk):
        z = cols(c_qa + half * chunk, chunk)
        for u in range(4):
            y = _group_rms(unit(z, u), qna_ref[...], HEAD_DIM)
            put(qa_ref, half * 4 + u, rope_a(y) * (HEAD_DIM ** -0.5 * LOG2E))

    z = cols(c_ka, W_KA + W_VA)
    for u in range(ATTN_KV_HEADS):
        y = _group_rms(unit(z, u), kna_ref[...], HEAD_DIM)
        if not latent:
            put(ska_ref, u, y)
        put(ka_ref, u, rope_a(y))
        v = unit(z, ATTN_KV_HEADS + u)
        if not latent:
            put(sva_ref, u, v)
        put(va_ref, u, v)

    z = cols(c_vb, W_VB)
    if not latent:
        svb_ref[...] = z
    vb_ref[...] = z.astype(BF)

    f_ref[...] = cols(c_f, W_F).astype(BF)


def _proj(x, mods, l, norm_g, w_in, qna, kna, qnb, knb, tables, *, latent, tm):
    n_tok = x.shape[0]
    n_blk = n_tok // tm
    per = DEC_SEQ // tm
    row_fn = (lambda i: i // per) if latent else (lambda i: CTX_ROW)
    vec = lambda n: pl.BlockSpec((None, 1, n), lambda i: (l, 0, 0))
    in_specs = [
        pl.BlockSpec((tm, D_MODEL), lambda i: (i, 0)),
        _mod_spec(l, 0, row_fn),
        _mod_spec(l, 1, row_fn),
        vec(D_MODEL),
        _resident((None, D_MODEL, IN_WIDTH), lambda i: (l, 0, 0)),
        vec(LANES), vec(LANES), vec(LANES), vec(LANES),
    ]
    args = [x, mods, mods, norm_g, w_in, qna, kna, qnb, knb]
    widths = [W_QA, W_KA, W_VA, W_QB, W_KB, W_VB, W_F]
    out_specs = [pl.BlockSpec((tm, w), lambda i: (i, 0)) for w in widths]
    out_shape = [jax.ShapeDtypeStruct((n_tok, w), BF) for w in widths]
    if latent:
        in_specs += [pl.BlockSpec((tm, LANES), lambda i: (i % per, 0))] * 4
        args += list(tables)
    else:
        swidths = [W_KA, W_VA, W_KB, W_VB]
        out_specs += [pl.BlockSpec((tm, w), lambda i: (i, 0)) for w in swidths]
        out_shape += [jax.ShapeDtypeStruct((n_tok, w), F32) for w in swidths]
    return pl.pallas_call(
        functools.partial(_proj_kernel, latent),
        grid=(n_blk,),
        in_specs=in_specs,
        out_specs=out_specs,
        out_shape=out_shape,
        compiler_params=_cparams(1),
        name="proj_lat" if latent else "proj_ctx",
    )(*args)


def _scores(q, k):
    return lax.dot_general(q, k, (((1,), (1,)), ((), ())), preferred_element_type=F32)


def _with_ones(v):
    return jnp.concatenate([v, jnp.ones_like(v)], axis=1)


def _softmax_pv(q, ks, vs):
    ss = [_scores(q, k) for k in ks]
    m = functools.reduce(jnp.maximum, [jnp.max(s, axis=-1, keepdims=True) for s in ss])
    o = functools.reduce(
        jnp.add,
        [jnp.dot(jnp.exp2(s - m).astype(BF), v, preferred_element_type=F32) for s, v in zip(ss, vs)])
    return o[:, :LANES] / o[:, LANES:LANES + 1]


def _kv_sources(keys, unit, k_ref, v_ref, cache_refs):
    ks, vs = [k_ref[keys, unit]], [_with_ones(v_ref[keys, unit])]
    if cache_refs:
        kc_ref, vc_ref = cache_refs
        ks.append(kc_ref[:, unit].astype(BF))
        vs.append(_with_ones(vc_ref[:, unit].astype(BF)))
    return ks, vs


def _gqa_kernel(seq, q_ref, k_ref, v_ref, *refs):
    *cache_refs, o_ref = refs
    n_kv = k_ref.shape[1] // HEAD_DIM
    grp = q_ref.shape[1] // HEAD_DIM // n_kv
    for s0 in range(0, q_ref.shape[0], seq):
        for kv in range(n_kv):
            ks, vs = _kv_sources(slice(s0, s0 + seq), slice(kv * HEAD_DIM, (kv + 1) * HEAD_DIM),
                                 k_ref, v_ref, cache_refs)
            for h in range(kv * grp, (kv + 1) * grp):
                unit = slice(h * HEAD_DIM, (h + 1) * HEAD_DIM)
                for r in range(s0, s0 + seq, ATTN_ROWS):
                    rows = slice(r, r + ATTN_ROWS)
                    o_ref[rows, unit] = _softmax_pv(q_ref[rows, unit], ks, vs).astype(o_ref.dtype)


def _gqa(qa, ka, va, cache_k, cache_v, l, *, n_batch, seq):
    cached = cache_k is not None
    grp = ATTN_Q_HEADS // ATTN_KV_HEADS
    if cached:
        n_h = LAT_GQA_HEADS
        grid = (n_batch, ATTN_Q_HEADS // n_h)
        qspec = pl.BlockSpec((seq, n_h * HEAD_DIM), lambda b, h: (b, h))
        kvspec = pl.BlockSpec((seq, HEAD_DIM), lambda b, h: (b, h * n_h // grp))
        cspec = pl.BlockSpec((None, None, PAST_LEN, HEAD_DIM), lambda b, h: (b, l, 0, h * n_h // grp))
        in_specs = [qspec, kvspec, kvspec, cspec, cspec]
        args = [qa, ka, va, cache_k, cache_v]
    else:
        grid = (n_batch // CTX_ATTN_SEQS, 1)
        qspec = pl.BlockSpec((CTX_ATTN_SEQS * seq, W_QA), lambda b, h: (b, 0))
        kvspec = pl.BlockSpec((CTX_ATTN_SEQS * seq, W_KA), lambda b, h: (b, 0))
        in_specs = [qspec, kvspec, kvspec]
        args = [qa, ka, va]
    return pl.pallas_call(
        functools.partial(_gqa_kernel, seq),
        grid=grid,
        in_specs=in_specs,
        out_specs=qspec,
        out_shape=jax.ShapeDtypeStruct((n_batch * seq, W_QA), BF),
        compiler_params=_cparams(2),
        name="gqa_lat" if cached else "gqa_ctx",
    )(*args)


def _diff_kernel(seq, chain_rows, lambda_init, lq1_ref, lk1_ref, lq2_ref, lk2_ref, sg_ref, q_ref, k_ref, v_ref,
                 *refs):
    *cache_refs, o_ref = refs
    lam = (jnp.exp(jnp.sum(lq1_ref[...] * lk1_ref[...], axis=-1, keepdims=True))
           - jnp.exp(jnp.sum(lq2_ref[...] * lk2_ref[...], axis=-1, keepdims=True))
           + lambda_init)
    lo = lax.broadcasted_iota(jnp.int32, (1, LANES), 1) < DIFF_QK_DIM
    for s0 in range(0, q_ref.shape[0], seq):
        for h in range(q_ref.shape[1] // LANES):
            unit = slice(h * LANES, (h + 1) * LANES)
            ks, vs = _kv_sources(slice(s0, s0 + seq), unit, k_ref, v_ref, cache_refs)
            for r in range(s0, s0 + seq, chain_rows):
                rows = slice(r, r + chain_rows)
                q = q_ref[rows, unit]
                zero = jnp.zeros_like(q)
                o = _softmax_pv(jnp.where(lo, q, zero), ks, vs) - lam * _softmax_pv(jnp.where(lo, zero, q), ks, vs)
                ms = jnp.mean(o * o, axis=-1, keepdims=True)
                y = (o * lax.rsqrt(ms + EPS)) * sg_ref[...]
                o_ref[rows, unit] = (y * (1.0 - lambda_init)).astype(o_ref.dtype)


def _diff(qb, kb, vb, cache_k, cache_v, lams, subln_g, l, lambda_init, *, n_batch, seq):
    cached = cache_k is not None
    lvec = pl.BlockSpec((None, 1, DIFF_QK_DIM), lambda b, h: (l, 0, 0))
    in_specs = [lvec] * 4 + [pl.BlockSpec((None, 1, DIFF_V_DIM), lambda b, h: (l, 0, 0))]
    args = list(lams) + [subln_g, qb, kb, vb]
    if cached:
        grid = (n_batch, DIFF_HEADS)
        spec = pl.BlockSpec((seq, LANES), lambda b, h: (b, h))
        cspec = pl.BlockSpec((None, None, PAST_LEN, LANES), lambda b, h: (b, l, 0, h))
        in_specs += [spec] * 3 + [cspec] * 2
        args += [cache_k, cache_v]
    else:
        grid = (n_batch // CTX_ATTN_SEQS, 1)
        spec = pl.BlockSpec((CTX_ATTN_SEQS * seq, W_VB), lambda b, h: (b, 0))
        in_specs += [spec] * 3
    return pl.pallas_call(
        functools.partial(_diff_kernel, seq, DIFF_LAT_ROWS if cached else ATTN_ROWS, lambda_init),
        grid=grid,
        in_specs=in_specs,
        out_specs=spec,
        out_shape=jax.ShapeDtypeStruct((n_batch * seq, W_VB), BF),
        compiler_params=_cparams(2),
        name="diff_lat" if cached else "diff_ctx",
    )(*args)


def _fourier_kernel(scale, f_ref, ct_ref, st_ref, cc_ref, sc_ref, w_ref, o_ref):
    f = f_ref[...]
    g1 = jnp.dot(f, cc_ref[...], preferred_element_type=F32).astype(BF)
    g2 = jnp.dot(f, sc_ref[...], preferred_element_type=F32).astype(BF)
    spec = (jnp.dot(ct_ref[...], g1, preferred_element_type=F32)
            - jnp.dot(st_ref[...], g2, preferred_element_type=F32)) * scale
    sb = spec.astype(BF)
    c = FOURIER_GROUP_DIM
    for g in range(FOURIER_GROUPS):
        o_ref[:, g * c:(g + 1) * c] = jnp.dot(
            sb[:, g * c:(g + 1) * c], w_ref[g], preferred_element_type=F32).astype(o_ref.dtype)


def _fourier(f, ct, st, cc, sc, w_fourier, l, *, n_batch, seq):
    scale = 1.0 / math.sqrt(seq * FOURIER_GROUP_DIM)
    c = FOURIER_GROUP_DIM
    return pl.pallas_call(
        functools.partial(_fourier_kernel, scale),
        grid=(n_batch,),
        in_specs=[
            pl.BlockSpec((seq, W_F), lambda b: (b, 0)),
            _resident((seq, seq), lambda b: (0, 0)),
            _resident((seq, seq), lambda b: (0, 0)),
            _resident((W_F, W_F), lambda b: (0, 0)),
            _resident((W_F, W_F), lambda b: (0, 0)),
            _resident((None, FOURIER_GROUPS, c, c), lambda b: (l, 0, 0, 0)),
        ],
        out_specs=pl.BlockSpec((seq, W_F), lambda b: (b, 0)),
        out_shape=jax.ShapeDtypeStruct((n_batch * seq, W_F), BF),
        compiler_params=_cparams(1),
        name="fourier_lat" if seq == DEC_SEQ else "fourier_ctx",
    )(f, ct, st, cc, sc, w_fourier)


def _dft_tables(n):
    k = np.arange(n, dtype=np.int64)
    ang = (2.0 * np.pi / n) * ((k[:, None] * k[None, :]) % n).astype(np.float64)
    return np.cos(ang), np.sin(ang)


def _oproj_kernel(x_ref, gt_ref, a_ref, b_ref, c_ref, w_ref, o_ref):
    acc = jnp.dot(a_ref[...], w_ref[0:W_QA, :], preferred_element_type=F32)
    acc += jnp.dot(b_ref[...], w_ref[W_QA:W_QA + W_VB, :], preferred_element_type=F32)
    acc += jnp.dot(c_ref[...], w_ref[W_QA + W_VB:MIX_WIDTH, :], preferred_element_type=F32)
    o_ref[...] = x_ref[...] + gt_ref[...] * acc


def _oproj(x, mods, l, out_a, out_b, out_c, w_out, *, latent, tm):
    n_tok = x.shape[0]
    per = DEC_SEQ // tm
    row_fn = (lambda i: i // per) if latent else (lambda i: CTX_ROW)
    return pl.pallas_call(
        _oproj_kernel,
        grid=(n_tok // tm,),
        in_specs=[
            pl.BlockSpec((tm, D_MODEL), lambda i: (i, 0)),
            _mod_spec(l, 2, row_fn),
            pl.BlockSpec((tm, W_QA), lambda i: (i, 0)),
            pl.BlockSpec((tm, W_VB), lambda i: (i, 0)),
            pl.BlockSpec((tm, W_F), lambda i: (i, 0)),
            _resident((None, MIX_WIDTH, D_MODEL), lambda i: (l, 0, 0)),
        ],
        out_specs=pl.BlockSpec((tm, D_MODEL), lambda i: (i, 0)),
        out_shape=jax.ShapeDtypeStruct((n_tok, D_MODEL), F32),
        compiler_params=_cparams(1),
        name="oproj_lat" if latent else "oproj_ctx",
    )(x, mods, out_a, out_b, out_c, w_out)


def _mlp_kernel(x_ref, sh_ref, sc_ref, gt_ref, g_ref, w1_ref, w2_ref, o_ref, h_scr):
    j = pl.program_id(1)

    @pl.when(j == 0)
    def _():
        _modulated_norm_store(x_ref, g_ref, sc_ref, sh_ref, h_scr)
        o_ref[...] = jnp.zeros_like(o_ref)

    hid = jnp.maximum(jnp.dot(h_scr[...], w1_ref[...], preferred_element_type=F32), 0.0)
    o_ref[...] += jnp.dot((hid * hid).astype(BF), w2_ref[...], preferred_element_type=F32)

    @pl.when(j == pl.num_programs(1) - 1)
    def _():
        o_ref[...] = x_ref[...] + gt_ref[...] * o_ref[...]


def _mlp(x, mods, l, norm_g, w1, w2, *, latent, tm, tf):
    n_tok = x.shape[0]
    per = DEC_SEQ // tm
    row_fn = (lambda i: i // per) if latent else (lambda i: CTX_ROW)
    return pl.pallas_call(
        _mlp_kernel,
        grid=(n_tok // tm, D_FF // tf),
        in_specs=[
            pl.BlockSpec((tm, D_MODEL), lambda i, j: (i, 0)),
            _mod_spec(l, 3, row_fn),
            _mod_spec(l, 4, row_fn),
            _mod_spec(l, 5, row_fn),
            pl.BlockSpec((None, 1, D_MODEL), lambda i, j: (l, 0, 0)),
            pl.BlockSpec((None, D_MODEL, tf), lambda i, j: (l, 0, j)),
            pl.BlockSpec((None, tf, D_MODEL), lambda i, j: (l, j, 0)),
        ],
        out_specs=pl.BlockSpec((tm, D_MODEL), lambda i, j: (i, 0)),
        out_shape=jax.ShapeDtypeStruct((n_tok, D_MODEL), F32),
        scratch_shapes=[pltpu.VMEM((tm, D_MODEL), BF)],
        compiler_params=_cparams(2),
        name="mlp_lat" if latent else "mlp_ctx",
    )(x, mods, mods, mods, norm_g, w1, w2)


def _rope_tables(n_tokens, rot_dim):
    rows = n_tokens // GRID_W
    row = jnp.repeat(jnp.arange(rows), GRID_W).astype(F32)
    col = jnp.tile(jnp.arange(GRID_W), rows).astype(F32)
    nf = rot_dim // 4
    inv = ROPE_THETA ** (-jnp.arange(nf, dtype=F32) / nf)
    ar, ac = row[:, None] * inv, col[:, None] * inv
    cos = jnp.concatenate([jnp.cos(ar), jnp.cos(ar), jnp.cos(ac), jnp.cos(ac)], axis=-1)
    sin = jnp.concatenate([-jnp.sin(ar), jnp.sin(ar), -jnp.sin(ac), jnp.sin(ac)], axis=-1)
    reps = LANES // rot_dim
    return jnp.tile(cos, (1, reps)), jnp.tile(sin, (1, reps))


def _block_diag(m, n):
    return np.kron(np.eye(n), m)


def kernel(x_prompt, x_sample, cache_attn_k, cache_attn_v, cache_diff_k, cache_diff_v, c, c_ctx, w_ada, b_ada, norm_mix_g, norm_mlp_g, w_in, q_norm_a, k_norm_a, q_norm_b, k_norm_b, lambda_q1, lambda_k1, lambda_q2, lambda_k2, subln_g, w_fourier, w_out, w_mlp_in, w_mlp_out):
    n_ctx, n_lat = BATCH * SEQ, DEC_BATCH * DEC_SEQ
    xp = x_prompt.reshape(n_ctx, D_MODEL)
    xs = x_sample.reshape(n_lat, D_MODEL)

    cond = jnp.zeros((N_COND, D_MODEL), F32).at[:DEC_BATCH].set(c).at[CTX_ROW].set(c_ctx)
    mods = _adaln_all(cond, w_ada, b_ada).reshape(DEPTH, N_COND, N_MOD, 1, D_MODEL)

    w_in_b = w_in.astype(BF)
    w_out_b = w_out.astype(BF)
    w1_b = w_mlp_in.astype(BF)
    w2_b = w_mlp_out.astype(BF)
    wf_b = w_fourier.astype(BF)

    row3 = lambda a: a.reshape(DEPTH, 1, a.shape[-1])
    norm_mix = row3(norm_mix_g)
    norm_mlp = row3(norm_mlp_g)
    qna, kna = row3(q_norm_a), row3(k_norm_a)
    qnb = row3(jnp.tile(q_norm_b, (1, LANES // DIFF_QK_DIM)))
    knb = row3(jnp.tile(k_norm_b, (1, LANES // DIFF_QK_DIM)))
    lams = [row3(a) for a in (lambda_q1, lambda_k1, lambda_q2, lambda_k2)]
    subg = row3(subln_g)

    tables = _rope_tables(DEC_SEQ, HEAD_DIM) + _rope_tables(DEC_SEQ, DIFF_QK_DIM)

    cch, sch = _dft_tables(FOURIER_GROUP_DIM)
    cc = jnp.asarray(_block_diag(cch, FOURIER_GROUPS), BF)
    sc = jnp.asarray(_block_diag(sch, FOURIER_GROUPS), BF)
    dft = {n: tuple(jnp.asarray(t, BF) for t in _dft_tables(n)) for n in (SEQ, DEC_SEQ)}

    ck_a = cache_attn_k.reshape(DEC_BATCH, DEPTH, PAST_LEN, W_KA)
    cv_a = cache_attn_v.reshape(DEC_BATCH, DEPTH, PAST_LEN, W_VA)
    ck_b = cache_diff_k.reshape(DEC_BATCH, DEPTH, PAST_LEN, W_KB)
    cv_b = cache_diff_v.reshape(DEC_BATCH, DEPTH, PAST_LEN, W_VB)

    def layer(x, l, latent):
        lambda_init = 0.8 - 0.6 * math.exp(-0.3 * l)
        n_batch, seq = (DEC_BATCH, DEC_SEQ) if latent else (BATCH, SEQ)
        rows = _block_rows(latent)
        outs = _proj(x, mods, l, norm_mix, w_in_b, qna, kna, qnb, knb, tables, latent=latent, tm=rows["proj"])
        qa, ka, va, qb, kb, vb, f = outs[:7]
        caches = (ck_a, cv_a, ck_b, cv_b) if latent else (None,) * 4
        out_a = _gqa(qa, ka, va, caches[0], caches[1], l, n_batch=n_batch, seq=seq)
        out_b = _diff(qb, kb, vb, caches[2], caches[3], lams, subg, l, lambda_init, n_batch=n_batch, seq=seq)
        out_c = _fourier(f, dft[seq][0], dft[seq][1], cc, sc, wf_b, l, n_batch=n_batch, seq=seq)
        x = _oproj(x, mods, l, out_a, out_b, out_c, w_out_b, latent=latent, tm=rows["oproj"])
        x = _mlp(x, mods, l, norm_mlp, w1_b, w2_b, latent=latent, tm=rows["mlp"], tf=MLP_FF_BLOCK)
        return x, outs[7:]

    states = []
    for l in range(DEPTH):
        xp, st = layer(xp, l, False)
        states.append(st)
        xs, _ = layer(xs, l, True)

    def stack(k, shape):
        return jnp.stack([s[k].reshape((BATCH, SEQ) + shape) for s in states], axis=1)

    return (
        xp.reshape(BATCH, SEQ, D_MODEL),
        xs.reshape(DEC_BATCH, DEC_SEQ, D_MODEL),
        stack(0, (ATTN_KV_HEADS, HEAD_DIM)),
        stack(1, (ATTN_KV_HEADS, HEAD_DIM)),
        stack(2, (DIFF_HEADS, 2, DIFF_QK_DIM)),
        stack(3, (DIFF_HEADS, DIFF_V_DIM)),
    )
```
